```python
import jax, jax.numpy as jnp
from jax import lax
import numpy as np

D_MODEL = 1024
BATCH = 2
SEQ = 16384
DEPTH = 2
DEC_BATCH = 8
DEC_SEQ = 32
PAST_LEN = 1024

CHUNK = 64
N_MIXERS = 2
N_POOL_LAYERS = (DEPTH + 1) // 2
N_SSD_LAYERS = DEPTH // 2
ALPHA = (2 * DEPTH) ** 0.25
BETA = (8 * DEPTH) ** -0.25
LN_EPS = 1e-5
POOL_WINDOWS = (2, 4, 8, 16)
N_POOL_GROUPS = len(POOL_WINDOWS)
POOL_GROUP = D_MODEL // N_POOL_GROUPS
POOL_HIST = max(POOL_WINDOWS) - 1
SSM_EXPAND = 2
D_INNER = SSM_EXPAND * D_MODEL
HEAD_DIM = 64
N_SSM_HEADS = D_INNER // HEAD_DIM
N_SSM_GROUPS = 8
HEADS_PER_GROUP = N_SSM_HEADS // N_SSM_GROUPS
D_STATE = 128
SSM_CONV = 4
CONV_DIM = D_INNER + 2 * N_SSM_GROUPS * D_STATE
IN_PROJ_DIM = D_INNER + CONV_DIM + N_SSM_HEADS
SSD_BLOCK = CHUNK
RMS_EPS = 1e-5
D_FF = 2816
FFN_CONV = 3
PLE_DIM = 256

kernel_name = "pool_ssd_hybrid_stream_step"


def layer_norm(x, g, b):
    xf = x.astype(jnp.float32)
    mu = jnp.mean(xf, axis=-1, keepdims=True)
    xc = xf - mu
    var = jnp.mean(xc * xc, axis=-1, keepdims=True)
    return (xc * lax.rsqrt(var + LN_EPS) * g.astype(jnp.float32) + b.astype(jnp.float32)).astype(x.dtype)


def causal_dwconv(xh, w, b):
    c = xh.shape[-1]
    y = lax.conv_general_dilated(xh, w[:, None, :].astype(xh.dtype), window_strides=(1,), padding='VALID',
                                 dimension_numbers=('NWC', 'WIO', 'NWC'), feature_group_count=c)
    return y + b.astype(xh.dtype)


def pool_mixer(x, hist, pos0, w, scale):
    bsz, L, _ = x.shape
    xh = jnp.concatenate([hist.astype(x.dtype), x], axis=1)
    cs = jnp.pad(jnp.cumsum(xh.astype(jnp.float32), axis=1), ((0, 0), (1, 0), (0, 0)))
    end = cs[:, POOL_HIST + 1:POOL_HIST + 1 + L]
    pos = pos0 + jnp.arange(L)
    outs = []
    for gi, wsz in enumerate(POOL_WINDOWS):
        sl = slice(gi * POOL_GROUP, (gi + 1) * POOL_GROUP)
        s = end[..., sl] - cs[:, POOL_HIST + 1 - wsz:POOL_HIST + 1 - wsz + L, sl]
        cnt = jnp.minimum(wsz, pos + 1).astype(jnp.float32)[None, :, None]
        outs.append(s / cnt)
    pooled = (jnp.concatenate(outs, axis=-1) - x.astype(jnp.float32)).astype(x.dtype)
    g = pooled.reshape(bsz, L, N_POOL_GROUPS, POOL_GROUP)
    y = jnp.einsum('blgc,gcd->blgd', g, w).reshape(bsz, L, D_MODEL) * scale
    return y.astype(x.dtype), xh[:, -POOL_HIST:]


def ssd_scan(xs, dt, A, Bm, Cm, S0, q):
    bsz, L = xs.shape[:2]
    nc = L // q
    f32 = jnp.float32

    def to_blocks(t):
        return jnp.moveaxis(t.reshape(bsz, nc, q, *t.shape[2:]), 1, 0)

    xdt = xs.astype(f32) * dt[..., None]
    a = dt * A
    causal = jnp.tril(jnp.ones((q, q), dtype=bool))[None, :, :, None, None]

    def step(S, inp):
        xc, ac, bc, cc = inp
        acs = jnp.cumsum(ac, axis=1)
        seg = acs[:, :, None] - acs[:, None, :]
        Lm = jnp.exp(jnp.where(causal, seg, -jnp.inf))
        cb = jnp.einsum('bqgn,bsgn->bqsg', cc, bc)
        y = jnp.einsum('bqsg,bqsgr,bsgrp->bqgrp', cb, Lm, xc)
        y = y + jnp.einsum('bqgn,bgrpn->bqgrp', cc, S) * jnp.exp(acs)[..., None]
        decay = jnp.exp(acs[:, -1:] - acs)
        S = S * jnp.exp(acs[:, -1])[..., None, None] + jnp.einsum('bsgn,bsgr,bsgrp->bgrpn', bc, decay, xc)
        return S, y

    S, ys = lax.scan(step, S0.astype(f32),
                     (to_blocks(xdt), to_blocks(a), to_blocks(Bm.astype(f32)), to_blocks(Cm.astype(f32))))
    y = jnp.moveaxis(ys, 0, 1).reshape(xs.shape)
    return y, S


def ssm_mixer(x, conv_hist, S0, in_proj, conv_w, conv_b, dt_bias, A_log, D_skip, norm_w, out_proj, q):
    bsz, L, _ = x.shape
    f32 = jnp.float32
    zxbcdt = x @ in_proj
    z = zxbcdt[..., :D_INNER]
    xBC = zxbcdt[..., D_INNER:D_INNER + CONV_DIM]
    dt_raw = zxbcdt[..., D_INNER + CONV_DIM:]
    xBC_h = jnp.concatenate([conv_hist.astype(x.dtype), xBC], axis=1)
    xBC_c = jax.nn.silu(causal_dwconv(xBC_h, conv_w, conv_b))
    new_conv = xBC_h[:, -(SSM_CONV - 1):]
    xs = xBC_c[..., :D_INNER].reshape(bsz, L, N_SSM_GROUPS, HEADS_PER_GROUP, HEAD_DIM)
    Bm = xBC_c[..., D_INNER:D_INNER + N_SSM_GROUPS * D_STATE].reshape(bsz, L, N_SSM_GROUPS, D_STATE)
    Cm = xBC_c[..., D_INNER + N_SSM_GROUPS * D_STATE:].reshape(bsz, L, N_SSM_GROUPS, D_STATE)
    dt = jax.nn.softplus(dt_raw.astype(f32) + dt_bias.astype(f32)).reshape(bsz, L, N_SSM_GROUPS, HEADS_PER_GROUP)
    A = -jnp.exp(A_log.astype(f32)).reshape(N_SSM_GROUPS, HEADS_PER_GROUP)
    S0 = S0.reshape(bsz, N_SSM_GROUPS, HEADS_PER_GROUP, HEAD_DIM, D_STATE)
    y, S = ssd_scan(xs, dt, A, Bm, Cm, S0, q)
    y = y + D_skip.astype(f32).reshape(N_SSM_GROUPS, HEADS_PER_GROUP)[..., None] * xs.astype(f32)
    y = y.reshape(bsz, L, D_INNER) * jax.nn.silu(z.astype(f32))
    yg = y.reshape(bsz, L, N_SSM_GROUPS, D_INNER // N_SSM_GROUPS)
    yg = yg * lax.rsqrt(jnp.mean(yg * yg, axis=-1, keepdims=True) + RMS_EPS)
    y = (yg.reshape(bsz, L, D_INNER) * norm_w.astype(f32)).astype(x.dtype)
    out = y @ out_proj
    return out, new_conv, S.reshape(bsz, N_SSM_HEADS, HEAD_DIM, D_STATE).astype(x.dtype)


def conv_ffn(x, hist, w_up, conv_w, conv_b, w_down):
    h = x @ w_up
    hh = jnp.concatenate([hist.astype(x.dtype), h], axis=1)
    hc = causal_dwconv(hh, conv_w, conv_b)
    g, u = hc[..., :D_FF], hc[..., D_FF:]
    y = (jax.nn.silu(g) * u) @ w_down
    return y, hh[:, -(FFN_CONV - 1):]


def trunk(x, p, pool_hist, ssm_conv_hist, ssm_state, ffn_hist, pos0, ssd_q, w):
    new_pool, new_sconv, new_sstate, new_ffn = [], [], [], []
    for i in range(DEPTH):
        j = i // N_MIXERS
        if i % N_MIXERS == 0:
            m, ph = pool_mixer(x, pool_hist[j], pos0, w['pool_w'][j], w['pool_scale'][j])
            new_pool.append(ph)
        else:
            m, ch, S = ssm_mixer(x, ssm_conv_hist[j], ssm_state[j], w['ssm_in_proj'][j], w['ssm_conv_w'][j],
                                 w['ssm_conv_b'][j], w['ssm_dt_bias'][j], w['ssm_A_log'][j], w['ssm_D'][j],
                                 w['ssm_norm_w'][j], w['ssm_out_proj'][j], ssd_q)
            new_sconv.append(ch)
            new_sstate.append(S)
        x = layer_norm(ALPHA * x + m, w['ln_mix_g'][i], w['ln_mix_b'][i])
        f, fh = conv_ffn(x, ffn_hist[i], w['ffn_up'][i], w['ffn_conv_w'][i], w['ffn_conv_b'][i], w['ffn_down'][i])
        new_ffn.append(fh)
        x = layer_norm(ALPHA * x + f, w['ln_ffn_g'][i], w['ln_ffn_b'][i])
        gate = jax.nn.sigmoid(x @ w['ple_gate_w'][i] + w['ple_gate_b'][i])
        x = x + gate * (p[i] @ w['ple_proj'][i])
    return x, jnp.stack(new_pool), jnp.stack(new_sconv), jnp.stack(new_sstate), jnp.stack(new_ffn)


def setup_inputs(seed: int = 0) -> dict:
    key = jax.random.key(seed)
    ks = iter(jax.random.split(key, 40))
    nrm = lambda shape, s=1.0: jax.random.normal(next(ks), shape, jnp.float32) * s
    NP, NS = N_POOL_LAYERS, N_SSD_LAYERS
    dt0 = jnp.exp(jax.random.uniform(next(ks), (NS, N_SSM_HEADS), jnp.float32, np.log(1e-3), np.log(1e-1)))
    dt_bias = dt0 + jnp.log(-jnp.expm1(-dt0))
    A_log = jnp.log(jax.random.uniform(next(ks), (NS, N_SSM_HEADS), jnp.float32, 1.0, 16.0))
    return {
        'x_prompt': nrm((BATCH, SEQ, D_MODEL)),
        'x_sample': nrm((DEC_BATCH, DEC_SEQ, D_MODEL)),
        'p_prompt': nrm((DEPTH, BATCH, SEQ, PLE_DIM)),
        'p_sample': nrm((DEPTH, DEC_BATCH, DEC_SEQ, PLE_DIM)),
        'cache_pool': nrm((NP, DEC_BATCH, POOL_HIST, D_MODEL)),
        'cache_ssm_conv': nrm((NS, DEC_BATCH, SSM_CONV - 1, CONV_DIM)),
        'state_ssm': nrm((NS, DEC_BATCH, N_SSM_HEADS, HEAD_DIM, D_STATE), 0.1),
        'cache_ffn_conv': nrm((DEPTH, DEC_BATCH, FFN_CONV - 1, 2 * D_FF)),
        'pool_w': nrm((NP, N_POOL_GROUPS, POOL_GROUP, POOL_GROUP), BETA * POOL_GROUP ** -0.5),
        'pool_scale': 1.0 + nrm((NP, D_MODEL), 0.02),
        'ssm_in_proj': nrm((NS, D_MODEL, IN_PROJ_DIM), D_MODEL ** -0.5),
        'ssm_conv_w': nrm((NS, SSM_CONV, CONV_DIM), SSM_CONV ** -0.5),
        'ssm_conv_b': nrm((NS, CONV_DIM), 0.02),
        'ssm_dt_bias': dt_bias,
        'ssm_A_log': A_log,
        'ssm_D': 1.0 + nrm((NS, N_SSM_HEADS), 0.02),
        'ssm_norm_w': 1.0 + nrm((NS, D_INNER), 0.02),
        'ssm_out_proj': nrm((NS, D_INNER, D_MODEL), BETA * D_INNER ** -0.5),
        'ln_mix_g': 1.0 + nrm((DEPTH, D_MODEL), 0.02),
        'ln_mix_b': nrm((DEPTH, D_MODEL), 0.02),
        'ffn_up': nrm((DEPTH, D_MODEL, 2 * D_FF), D_MODEL ** -0.5),
        'ffn_conv_w': nrm((DEPTH, FFN_CONV, 2 * D_FF), FFN_CONV ** -0.5),
        'ffn_conv_b': nrm((DEPTH, 2 * D_FF), 0.02),
        'ffn_down': nrm((DEPTH, D_FF, D_MODEL), BETA * D_FF ** -0.5),
        'ln_ffn_g': 1.0 + nrm((DEPTH, D_MODEL), 0.02),
        'ln_ffn_b': nrm((DEPTH, D_MODEL), 0.02),
        'ple_proj': nrm((DEPTH, PLE_DIM, D_MODEL), BETA * PLE_DIM ** -0.5),
        'ple_gate_w': nrm((DEPTH, D_MODEL, D_MODEL), D_MODEL ** -0.5),
        'ple_gate_b': nrm((DEPTH, D_MODEL), 0.02),
    }


def reference(x_prompt, x_sample, p_prompt, p_sample, cache_pool, cache_ssm_conv, state_ssm, cache_ffn_conv,
              pool_w, pool_scale, ssm_in_proj, ssm_conv_w, ssm_conv_b, ssm_dt_bias, ssm_A_log, ssm_D, ssm_norm_w,
              ssm_out_proj, ln_mix_g, ln_mix_b, ffn_up, ffn_conv_w, ffn_conv_b, ffn_down, ln_ffn_g, ln_ffn_b,
              ple_proj, ple_gate_w, ple_gate_b):
    w = dict(pool_w=pool_w, pool_scale=pool_scale, ssm_in_proj=ssm_in_proj, ssm_conv_w=ssm_conv_w,
             ssm_conv_b=ssm_conv_b, ssm_dt_bias=ssm_dt_bias, ssm_A_log=ssm_A_log, ssm_D=ssm_D,
             ssm_norm_w=ssm_norm_w, ssm_out_proj=ssm_out_proj, ln_mix_g=ln_mix_g, ln_mix_b=ln_mix_b,
             ffn_up=ffn_up, ffn_conv_w=ffn_conv_w, ffn_conv_b=ffn_conv_b, ffn_down=ffn_down,
             ln_ffn_g=ln_ffn_g, ln_ffn_b=ln_ffn_b, ple_proj=ple_proj, ple_gate_w=ple_gate_w,
             ple_gate_b=ple_gate_b)
    bp = x_prompt.shape[0]
    dt_ = x_prompt.dtype
    z_pool = jnp.zeros((N_POOL_LAYERS, bp, POOL_HIST, D_MODEL), dt_)
    z_sconv = jnp.zeros((N_SSD_LAYERS, bp, SSM_CONV - 1, CONV_DIM), dt_)
    z_state = jnp.zeros((N_SSD_LAYERS, bp, N_SSM_HEADS, HEAD_DIM, D_STATE), dt_)
    z_ffn = jnp.zeros((DEPTH, bp, FFN_CONV - 1, 2 * D_FF), dt_)
    y_prompt, pool_p, sconv_p, state_p, ffn_p = trunk(x_prompt, p_prompt, z_pool, z_sconv, z_state, z_ffn,
                                                      0, SSD_BLOCK, w)
    y_sample, pool_s, sconv_s, state_s, ffn_s = trunk(x_sample, p_sample, cache_pool, cache_ssm_conv, state_ssm,
                                                      cache_ffn_conv, PAST_LEN, x_sample.shape[1], w)
    return (y_prompt, y_sample, pool_p, pool_s, sconv_p, sconv_s, state_p, state_s, ffn_p, ffn_s)
```

```python
import functools

import jax
import jax.numpy as jnp
from jax import lax
from jax.experimental import pallas as pl
from jax.experimental.pallas import tpu as pltpu

PAST_LEN = 1024
DEPTH = 2
ALPHA = (2 * DEPTH) ** 0.25
LN_EPS = 1e-5
RMS_EPS = 1e-5
POOL_WINDOWS = (2, 4, 8, 16)
POOL_HIST = max(POOL_WINDOWS) - 1
HEAD_DIM = 64
N_SSM_GROUPS = 8
HEADS_PER_GROUP = 4
N_SSM_HEADS = N_SSM_GROUPS * HEADS_PER_GROUP
D_STATE = 128
SSM_CONV = 4
FFN_CONV = 3
SSD_BLOCK = 64

LANES = 128
SUBLANES = 8
VMEM_LIMIT_BYTES = 56 * 1024 * 1024

HIST_ROWS = 2 * SUBLANES
CONV_ROWS = SUBLANES
GROUP_W = HEADS_PER_GROUP * HEAD_DIM


def _bdot(a, b):
    return jnp.dot(a.astype(jnp.bfloat16), b.astype(jnp.bfloat16), preferred_element_type=jnp.float32)


def _sigmoid(v):
    return 1.0 / (1.0 + jnp.exp(-v))


def _silu(v):
    return v * _sigmoid(v)


def _layer_norm(v, g, b):
    mu = jnp.mean(v, axis=-1, keepdims=True)
    vc = v - mu
    var = jnp.mean(vc * vc, axis=-1, keepdims=True)
    return vc * lax.rsqrt(var + LN_EPS) * g + b


def _const_spec(shape):
    nd = len(shape)
    return pl.BlockSpec(shape, lambda b, l: (0,) * nd, pipeline_mode=pl.Buffered(1))


def _split3(v):
    hi = v.astype(jnp.bfloat16).astype(jnp.float32)
    r1 = v - hi
    mid = r1.astype(jnp.bfloat16).astype(jnp.float32)
    lo = (r1 - mid).astype(jnp.bfloat16).astype(jnp.float32)
    return hi, mid, lo


def _pool_kernel(x_ref, hist_ref, w_ref, scale_ref, g_ref, b_ref, out_ref, newhist_ref, xh_ref,
                 *, nb, tl, d, pos0):
    l = pl.program_id(1)
    nl = pl.num_programs(1)
    gw = d // len(POOL_WINDOWS)

    @pl.when(l == 0)
    def _():
        xh_ref[:, 0:HIST_ROWS, :] = hist_ref[...]

    x = x_ref[...]
    xh_ref[:, HIST_ROWS:HIST_ROWS + tl, :] = x

    pos = pos0 + l * tl + lax.broadcasted_iota(jnp.int32, (1, tl, gw), 1)
    ys = []
    for gi, wsz in enumerate(POOL_WINDOWS):
        cols = slice(gi * gw, (gi + 1) * gw)
        s = x[:, :, cols]
        for k in range(1, wsz):
            s = s + xh_ref[:, HIST_ROWS - k:HIST_ROWS - k + tl, cols]
        cnt = jnp.minimum(wsz, pos + 1).astype(jnp.float32)
        pooled = (s / cnt - x[:, :, cols]).reshape(nb * tl, gw)
        ys.append(_bdot(pooled, w_ref[gi]))
    y = jnp.concatenate(ys, axis=-1) * scale_ref[...]
    xf = x.reshape(nb * tl, d)
    out = _layer_norm(ALPHA * xf + y, g_ref[...], b_ref[...])
    out_ref[...] = out.reshape(nb, tl, d)

    @pl.when(l == nl - 1)
    def _():
        newhist_ref[...] = xh_ref[:, HIST_ROWS + tl - POOL_HIST:HIST_ROWS + tl, :]

    xh_ref[:, 0:HIST_ROWS, :] = xh_ref[:, tl:tl + HIST_ROWS, :]


def _pool_layer(x, hist, w, scale, g, b, *, pos0, nb, tl):
    bt, seq, d = x.shape
    assert bt % nb == 0 and seq % tl == 0 and tl % HIST_ROWS == 0
    hist16 = jnp.pad(hist, ((0, 0), (HIST_ROWS - POOL_HIST, 0), (0, 0)))
    kern = functools.partial(_pool_kernel, nb=nb, tl=tl, d=d, pos0=pos0)
    return pl.pallas_call(
        kern,
        grid=(bt // nb, seq // tl),
        in_specs=[
            pl.BlockSpec((nb, tl, d), lambda b, l: (b, l, 0)),
            pl.BlockSpec((nb, HIST_ROWS, d), lambda b, l: (b, 0, 0)),
            _const_spec(w.shape),
            _const_spec((1, d)), _const_spec((1, d)), _const_spec((1, d)),
        ],
        out_specs=[
            pl.BlockSpec((nb, tl, d), lambda b, l: (b, l, 0)),
            pl.BlockSpec((nb, POOL_HIST, d), lambda b, l: (b, 0, 0)),
        ],
        out_shape=[
            jax.ShapeDtypeStruct((bt, seq, d), jnp.float32),
            jax.ShapeDtypeStruct((bt, POOL_HIST, d), jnp.float32),
        ],
        scratch_shapes=[pltpu.VMEM((nb, HIST_ROWS + tl, d), jnp.float32)],
        compiler_params=pltpu.CompilerParams(
            dimension_semantics=("arbitrary", "arbitrary"), vmem_limit_bytes=VMEM_LIMIT_BYTES),
        name="pool_mixer",
    )(x, hist16, w.astype(jnp.bfloat16), scale.reshape(1, d), g.reshape(1, d), b.reshape(1, d))


def _ffn_kernel(x_ref, p_ref, hist_ref, wup_ref, cw_ref, cb_ref, wdown_ref, g_ref, b_ref,
                wg_ref, bg_ref, wp_ref, out_ref, newhist_ref, hist_s, work_g, work_u, acc_s,
                *, nb, tl, d, dff, fc):
    l = pl.program_id(1)
    nl = pl.num_programs(1)
    nc = dff // fc
    t = nb * tl
    lo = CONV_ROWS - (FFN_CONV - 1)

    @pl.when(l == 0)
    def _():
        for j in range(2 * nc):
            hist_s[j] = hist_ref[:, :, j * fc:(j + 1) * fc]

    x = x_ref[...].reshape(t, d)
    xb = x.astype(jnp.bfloat16)

    def conv(work, cols):
        c = cb_ref[:, cols]
        for k in range(FFN_CONV):
            c = c + cw_ref[k:k + 1, cols] * work[:, lo + k:lo + k + tl, :]
        return c

    for j in range(nc):
        cols_g = slice(j * fc, (j + 1) * fc)
        cols_u = slice(dff + j * fc, dff + (j + 1) * fc)
        work_g[:, 0:CONV_ROWS, :] = hist_s[j]
        work_u[:, 0:CONV_ROWS, :] = hist_s[nc + j]
        work_g[:, CONV_ROWS:, :] = _bdot(xb, wup_ref[0, j]).reshape(nb, tl, fc)
        work_u[:, CONV_ROWS:, :] = _bdot(xb, wup_ref[1, j]).reshape(nb, tl, fc)
        hist_s[j] = work_g[:, tl:tl + CONV_ROWS, :]
        hist_s[nc + j] = work_u[:, tl:tl + CONV_ROWS, :]

        @pl.when(l == nl - 1)
        def _():
            newhist_ref[:, :, cols_g] = work_g[:, tl + lo:tl + CONV_ROWS, :]
            newhist_ref[:, :, cols_u] = work_u[:, tl + lo:tl + CONV_ROWS, :]

        act = (_silu(conv(work_g, cols_g)) * conv(work_u, cols_u)).reshape(t, fc)
        part = _bdot(act, wdown_ref[j])
        if j == 0:
            acc_s[...] = part
        else:
            acc_s[...] += part

    x2 = _layer_norm(ALPHA * x + acc_s[...], g_ref[...], b_ref[...])
    gate = _sigmoid(_bdot(x2, wg_ref[...]) + bg_ref[...])
    pp = _bdot(p_ref[...].reshape(t, p_ref.shape[-1]), wp_ref[...])
    out_ref[...] = (x2 + gate * pp).reshape(nb, tl, d)


def _ffn_layer(x, p, hist, w_up, conv_w, conv_b, w_down, g, b, gate_w, gate_b, ple_proj, *, nb, tl):
    bt, seq, d = x.shape
    pd = p.shape[-1]
    dff = w_down.shape[0]
    fc = 2 * LANES
    assert bt % nb == 0 and seq % tl == 0 and tl % SUBLANES == 0 and dff % fc == 0
    nc = dff // fc
    hist8 = jnp.pad(hist, ((0, 0), (CONV_ROWS - (FFN_CONV - 1), 0), (0, 0)))
    wup = w_up.astype(jnp.bfloat16).reshape(d, 2, nc, fc).transpose(1, 2, 0, 3)
    wdown = w_down.astype(jnp.bfloat16).reshape(nc, fc, d)
    kern = functools.partial(_ffn_kernel, nb=nb, tl=tl, d=d, dff=dff, fc=fc)
    return pl.pallas_call(
        kern,
        grid=(bt // nb, seq // tl),
        in_specs=[
            pl.BlockSpec((nb, tl, d), lambda b, l: (b, l, 0)),
            pl.BlockSpec((nb, tl, pd), lambda b, l: (b, l, 0)),
            pl.BlockSpec((nb, CONV_ROWS, 2 * dff), lambda b, l: (b, 0, 0)),
            _const_spec(wup.shape),
            _const_spec((FFN_CONV, 2 * dff)), _const_spec((1, 2 * dff)),
            _const_spec(wdown.shape),
            _const_spec((1, d)), _const_spec((1, d)),
            _const_spec((d, d)), _const_spec((1, d)), _const_spec((pd, d)),
        ],
        out_specs=[
            pl.BlockSpec((nb, tl, d), lambda b, l: (b, l, 0)),
            pl.BlockSpec((nb, FFN_CONV - 1, 2 * dff), lambda b, l: (b, 0, 0)),
        ],
        out_shape=[
            jax.ShapeDtypeStruct((bt, seq, d), jnp.float32),
            jax.ShapeDtypeStruct((bt, FFN_CONV - 1, 2 * dff), jnp.float32),
        ],
        scratch_shapes=[
            pltpu.VMEM((2 * nc, nb, CONV_ROWS, fc), jnp.float32),
            pltpu.VMEM((nb, CONV_ROWS + tl, fc), jnp.float32),
            pltpu.VMEM((nb, CONV_ROWS + tl, fc), jnp.float32),
            pltpu.VMEM((nb * tl, d), jnp.float32),
        ],
        compiler_params=pltpu.CompilerParams(
            dimension_semantics=("arbitrary", "arbitrary"), vmem_limit_bytes=VMEM_LIMIT_BYTES),
        name="conv_ffn",
    )(x, p, hist8, wup, conv_w, conv_b.reshape(1, 2 * dff), wdown, g.reshape(1, d), b.reshape(1, d),
      gate_w.astype(jnp.bfloat16), gate_b.reshape(1, d), ple_proj.astype(jnp.bfloat16))


def _ssd_kernel(x_ref, hist_ref, s0_ref, wz_ref, wxbc_ref, wdt_ref, cw_ref, cb_ref, dtb_ref, alog_ref,
                dskip_ref, nw_ref, wout_ref, g_ref, b_ref, tri_ref, eq_ref, ep_ref,
                out_ref, newhist_ref, state_ref,
                work_s, xs_s, b_s, c_s, dt_s, acs_s, cfq_s, cfp_s, dtp_s, y_s, yb_s,
                *, nb, tl, d, q):
    l = pl.program_id(1)
    nl = pl.num_programs(1)
    t = nb * tl
    d_inner = N_SSM_HEADS * HEAD_DIM
    gn = N_SSM_GROUPS * D_STATE
    conv_dim = d_inner + 2 * gn
    lo = CONV_ROWS - (SSM_CONV - 1)
    cpb = tl // q
    qw = HEADS_PER_GROUP * q
    cc = 4 * LANES
    bf16 = jnp.bfloat16

    @pl.when(l == 0)
    def _():
        work_s[:, 0:CONV_ROWS, :] = hist_ref[...]
        state_ref[...] = s0_ref[...]

    x = x_ref[...].reshape(t, d)
    xb = x.astype(bf16)

    for jc in range(conv_dim // cc):
        cols = slice(jc * cc, (jc + 1) * cc)
        work_s[:, CONV_ROWS:, cols] = _bdot(xb, wxbc_ref[:, cols]).reshape(nb, tl, cc)
    for jc in range(conv_dim // cc):
        cols = slice(jc * cc, (jc + 1) * cc)
        c = cb_ref[:, cols]
        for k in range(SSM_CONV):
            c = c + cw_ref[k:k + 1, cols] * work_s[:, lo + k:lo + k + tl, cols]
        v = _silu(c).reshape(t, cc)
        if (jc + 1) * cc <= d_inner:
            xs_s[:, cols] = v
        elif (jc + 1) * cc <= d_inner + gn:
            b_s[:, jc * cc - d_inner:(jc + 1) * cc - d_inner] = v
        else:
            c_s[:, jc * cc - d_inner - gn:(jc + 1) * cc - d_inner - gn] = v

    @pl.when(l == nl - 1)
    def _():
        newhist_ref[...] = work_s[:, tl + lo:tl + CONV_ROWS, :]

    work_s[:, 0:CONV_ROWS, :] = work_s[:, tl:tl + CONV_ROWS, :]

    dt = _bdot(xb, wdt_ref[...]) + dtb_ref[...]
    dt = jnp.maximum(dt, 0.0) + jnp.log1p(jnp.exp(-jnp.abs(dt)))
    dt_s[...] = dt
    a_row = -jnp.exp(alog_ref[...])
    lane = lax.broadcasted_iota(jnp.int32, (q, LANES), 1)

    def lane_split(v):
        hi, mid, lo_ = _split3(v)
        return jnp.where(lane < N_SSM_HEADS, hi, jnp.where(lane < 2 * N_SSM_HEADS, mid, lo_))

    for i in range(nb * cpb):
        rows = slice(i * q, (i + 1) * q)
        a = dt_s[rows, :] * a_row
        hi, mid, lo_ = _split3(a)
        stacked = jnp.concatenate([hi, mid, lo_, jnp.zeros_like(hi)], axis=0).astype(bf16)
        acs = jnp.dot(tri_ref[...], stacked, preferred_element_type=jnp.float32)
        acs_s[rows, :] = lane_split(acs)
        dt_s[rows, :] = lane_split(dt_s[rows, :])
    acs3 = acs_s[...].astype(bf16)
    cfq_s[...] = jnp.dot(acs3, eq_ref[...], preferred_element_type=jnp.float32)
    if q != HEAD_DIM:
        cfp_s[...] = jnp.dot(acs3, ep_ref[...], preferred_element_type=jnp.float32)
    dtp_s[...] = jnp.dot(dt_s[...].astype(bf16), ep_ref[...], preferred_element_type=jnp.float32)
    cfp = cfq_s if q == HEAD_DIM else cfp_s

    row_q = lax.broadcasted_iota(jnp.int32, (q, qw), 0)
    lane_q = lax.broadcasted_iota(jnp.int32, (q, qw), 1) & (q - 1)
    eye_t = row_q == lane_q
    causal = lane_q <= row_q
    bd_mask = ((lax.broadcasted_iota(jnp.int32, (qw, GROUP_W), 0) >> (q.bit_length() - 1))
               == (lax.broadcasted_iota(jnp.int32, (qw, GROUP_W), 1) >> (HEAD_DIM.bit_length() - 1)))
    pad_rows = LANES - q

    def chunk_body(i, carry):
        n = i // cpb
        rows = pl.ds(pl.multiple_of(i * q, q), q)
        for g in range(N_SSM_GROUPS):
            ncols = slice(g * D_STATE, (g + 1) * D_STATE)
            pcols = slice(g * GROUP_W, (g + 1) * GROUP_W)
            bg = b_s[rows, ncols]
            cg = c_s[rows, ncols].astype(bf16)
            cf = cfq_s[rows, g * qw:(g + 1) * qw]
            cfg = cfp[rows, pcols]
            xdt = xs_s[rows, pcols] * dtp_s[rows, pcols]
            rowf = jnp.sum(jnp.where(eye_t, cf, 0.0), axis=0, keepdims=True)
            lm = jnp.where(causal, jnp.exp(cf - rowf), 0.0)
            b4 = jnp.concatenate([bg.astype(bf16)] * HEADS_PER_GROUP, axis=0)
            cb4 = lax.dot_general(cg, b4, (((1,), (1,)), ((), ())), preferred_element_type=jnp.float32)
            m = (cb4 * lm).astype(bf16)
            xbd = jnp.where(bd_mask, jnp.concatenate([xdt] * HEADS_PER_GROUP, axis=0), 0.0).astype(bf16)
            y_intra = jnp.dot(m, xbd, preferred_element_type=jnp.float32)
            st = state_ref[n, g]
            y_inter = jnp.dot(cg, st.astype(bf16), preferred_element_type=jnp.float32) * jnp.exp(cfg)
            y_s[rows, pcols] = y_intra + y_inter
            last = cfg[q - 1:q, :]
            xd = xdt * jnp.exp(last - cfg)
            if pad_rows:
                bgp = jnp.concatenate([bg, jnp.zeros((pad_rows, D_STATE), jnp.float32)], axis=0)
                xd = jnp.concatenate([xd, jnp.zeros((pad_rows, GROUP_W), jnp.float32)], axis=0)
            else:
                bgp = bg
            upd = jnp.dot(bgp.T.astype(bf16), xd.astype(bf16), preferred_element_type=jnp.float32)
            state_ref[n, g] = st * jnp.exp(last) + upd
        return carry

    lax.fori_loop(0, nb * cpb, chunk_body, 0)

    for g in range(N_SSM_GROUPS):
        pcols = slice(g * GROUP_W, (g + 1) * GROUP_W)
        yg = y_s[:, pcols] + dskip_ref[:, pcols] * xs_s[:, pcols]
        yg = yg * _silu(_bdot(xb, wz_ref[:, pcols]))
        yg = yg * lax.rsqrt(jnp.mean(yg * yg, axis=-1, keepdims=True) + RMS_EPS)
        yb_s[:, pcols] = (yg * nw_ref[:, pcols]).astype(bf16)
    mix = jnp.dot(yb_s[...], wout_ref[...], preferred_element_type=jnp.float32)
    out_ref[...] = _layer_norm(ALPHA * x + mix, g_ref[...], b_ref[...]).reshape(nb, tl, d)


def _expand_matrix(rep):
    k = lax.broadcasted_iota(jnp.int32, (LANES, N_SSM_HEADS * rep), 0)
    j = lax.broadcasted_iota(jnp.int32, (LANES, N_SSM_HEADS * rep), 1)
    return ((k < 3 * N_SSM_HEADS) & (k % N_SSM_HEADS == j // rep)).astype(jnp.bfloat16)


def _ssd_layer(x, conv_hist, state, in_proj, conv_w, conv_b, dt_bias, a_log, d_skip, norm_w, out_proj,
               g, b, *, q, nb, tl):
    bt, seq, d = x.shape
    d_inner = N_SSM_HEADS * HEAD_DIM
    conv_dim = d_inner + 2 * N_SSM_GROUPS * D_STATE
    assert bt % nb == 0 and seq % tl == 0 and tl % q == 0 and q % SUBLANES == 0 and q <= LANES
    assert q & (q - 1) == 0, "chunk length must be a power of two"
    t = nb * tl
    qw = HEADS_PER_GROUP * q
    bf16 = jnp.bfloat16

    wz = in_proj[:, :d_inner].astype(bf16)
    wxbc = in_proj[:, d_inner:d_inner + conv_dim].astype(bf16)
    wdt = in_proj[:, d_inner + conv_dim:].astype(bf16)
    pad = LANES - 3 * N_SSM_HEADS
    rep3 = lambda v: jnp.pad(jnp.concatenate([v] * 3, axis=-1), ((0, 0), (0, pad)))
    wdt3 = rep3(wdt)
    dtb3 = rep3(dt_bias.reshape(1, N_SSM_HEADS))
    alog3 = rep3(a_log.reshape(1, N_SSM_HEADS))
    dskip = jnp.repeat(d_skip, HEAD_DIM).reshape(1, d_inner)
    hist8 = jnp.pad(conv_hist, ((0, 0), (CONV_ROWS - (SSM_CONV - 1), 0), (0, 0)))
    s0 = state.reshape(bt, N_SSM_GROUPS, HEADS_PER_GROUP, HEAD_DIM, D_STATE)
    s0 = s0.transpose(0, 1, 4, 2, 3).reshape(bt, N_SSM_GROUPS, D_STATE, GROUP_W)
    tri = (lax.broadcasted_iota(jnp.int32, (q, 4 * q), 1) % q
           <= lax.broadcasted_iota(jnp.int32, (q, 4 * q), 0))
    tri = (tri & (lax.broadcasted_iota(jnp.int32, (q, 4 * q), 1) < 3 * q)).astype(bf16)
    eq = _expand_matrix(q)
    ep = _expand_matrix(HEAD_DIM)

    kern = functools.partial(_ssd_kernel, nb=nb, tl=tl, d=d, q=q)
    f32 = jnp.float32
    out, newhist, newstate = pl.pallas_call(
        kern,
        grid=(bt // nb, seq // tl),
        in_specs=[
            pl.BlockSpec((nb, tl, d), lambda b, l: (b, l, 0)),
            pl.BlockSpec((nb, CONV_ROWS, conv_dim), lambda b, l: (b, 0, 0)),
            pl.BlockSpec((nb, N_SSM_GROUPS, D_STATE, GROUP_W), lambda b, l: (b, 0, 0, 0)),
            _const_spec(wz.shape), _const_spec(wxbc.shape), _const_spec(wdt3.shape),
            _const_spec((SSM_CONV, conv_dim)), _const_spec((1, conv_dim)),
            _const_spec((1, LANES)), _const_spec((1, LANES)),
            _const_spec((1, d_inner)), _const_spec((1, d_inner)),
            _const_spec((d_inner, d)), _const_spec((1, d)), _const_spec((1, d)),
            _const_spec(tri.shape), _const_spec(eq.shape), _const_spec(ep.shape),
        ],
        out_specs=[
            pl.BlockSpec((nb, tl, d), lambda b, l: (b, l, 0)),
            pl.BlockSpec((nb, SSM_CONV - 1, conv_dim), lambda b, l: (b, 0, 0)),
            pl.BlockSpec((nb, N_SSM_GROUPS, D_STATE, GROUP_W), lambda b, l: (b, 0, 0, 0)),
        ],
        out_shape=[
            jax.ShapeDtypeStruct((bt, seq, d), f32),
            jax.ShapeDtypeStruct((bt, SSM_CONV - 1, conv_dim), f32),
            jax.ShapeDtypeStruct((bt, N_SSM_GROUPS, D_STATE, GROUP_W), f32),
        ],
        scratch_shapes=[
            pltpu.VMEM((nb, CONV_ROWS + tl, conv_dim), f32),
            pltpu.VMEM((t, d_inner), f32),
            pltpu.VMEM((t, N_SSM_GROUPS * D_STATE), f32),
            pltpu.VMEM((t, N_SSM_GROUPS * D_STATE), f32),
            pltpu.VMEM((t, LANES), f32),
            pltpu.VMEM((t, LANES), f32),
            pltpu.VMEM((t, N_SSM_GROUPS * qw), f32),
            pltpu.VMEM((t, d_inner) if q != HEAD_DIM else (SUBLANES, LANES), f32),
            pltpu.VMEM((t, d_inner), f32),
            pltpu.VMEM((t, d_inner), f32),
            pltpu.VMEM((t, d_inner), bf16),
        ],
        compiler_params=pltpu.CompilerParams(
            dimension_semantics=("arbitrary", "arbitrary"), vmem_limit_bytes=VMEM_LIMIT_BYTES),
        name="ssd_mixer",
    )(x, hist8, s0, wz, wxbc, wdt3, conv_w, conv_b.reshape(1, conv_dim), dtb3, alog3, dskip,
      norm_w.reshape(1, d_inner), out_proj.astype(bf16), g.reshape(1, d), b.reshape(1, d), tri, eq, ep)
    newstate = newstate.reshape(bt, N_SSM_GROUPS, D_STATE, HEADS_PER_GROUP, HEAD_DIM)
    newstate = newstate.transpose(0, 1, 3, 4, 2).reshape(bt, N_SSM_HEADS, HEAD_DIM, D_STATE)
    return out, newhist, newstate


def _trunk(x, p, pool_hist, ssm_conv_hist, ssm_state, ffn_hist, w, *, pos0, q, pool_blk, ssd_blk, ffn_blk):
    x, new_pool = _pool_layer(x, pool_hist[0], w['pool_w'][0], w['pool_scale'][0],
                              w['ln_mix_g'][0], w['ln_mix_b'][0], pos0=pos0, nb=pool_blk[0], tl=pool_blk[1])
    new_ffn = []
    x, fh = _ffn_layer(x, p[0], ffn_hist[0], w['ffn_up'][0], w['ffn_conv_w'][0], w['ffn_conv_b'][0],
                       w['ffn_down'][0], w['ln_ffn_g'][0], w['ln_ffn_b'][0], w['ple_gate_w'][0],
                       w['ple_gate_b'][0], w['ple_proj'][0], nb=ffn_blk[0], tl=ffn_blk[1])
    new_ffn.append(fh)
    x, new_conv, new_state = _ssd_layer(
        x, ssm_conv_hist[0], ssm_state[0], w['ssm_in_proj'][0], w['ssm_conv_w'][0], w['ssm_conv_b'][0],
        w['ssm_dt_bias'][0], w['ssm_A_log'][0], w['ssm_D'][0], w['ssm_norm_w'][0], w['ssm_out_proj'][0],
        w['ln_mix_g'][1], w['ln_mix_b'][1], q=q, nb=ssd_blk[0], tl=ssd_blk[1])
    x, fh = _ffn_layer(x, p[1], ffn_hist[1], w['ffn_up'][1], w['ffn_conv_w'][1], w['ffn_conv_b'][1],
                       w['ffn_down'][1], w['ln_ffn_g'][1], w['ln_ffn_b'][1], w['ple_gate_w'][1],
                       w['ple_gate_b'][1], w['ple_proj'][1], nb=ffn_blk[0], tl=ffn_blk[1])
    new_ffn.append(fh)
    return x, new_pool[None], new_conv[None], new_state[None], jnp.stack(new_ffn)


def _block_len(seq, target):
    tl = min(seq, target)
    while seq % tl:
        tl //= 2
    return tl


def kernel(x_prompt, x_sample, p_prompt, p_sample, cache_pool, cache_ssm_conv, state_ssm, cache_ffn_conv,
           pool_w, pool_scale, ssm_in_proj, ssm_conv_w, ssm_conv_b, ssm_dt_bias, ssm_A_log, ssm_D, ssm_norm_w,
           ssm_out_proj, ln_mix_g, ln_mix_b, ffn_up, ffn_conv_w, ffn_conv_b, ffn_down, ln_ffn_g, ln_ffn_b,
           ple_proj, ple_gate_w, ple_gate_b):
    assert pool_w.shape[0] == 1 and ssm_in_proj.shape[0] == 1 and ffn_up.shape[0] == DEPTH
    w = dict(pool_w=pool_w, pool_scale=pool_scale, ssm_in_proj=ssm_in_proj, ssm_conv_w=ssm_conv_w,
             ssm_conv_b=ssm_conv_b, ssm_dt_bias=ssm_dt_bias, ssm_A_log=ssm_A_log, ssm_D=ssm_D,
             ssm_norm_w=ssm_norm_w, ssm_out_proj=ssm_out_proj, ln_mix_g=ln_mix_g, ln_mix_b=ln_mix_b,
             ffn_up=ffn_up, ffn_conv_w=ffn_conv_w, ffn_conv_b=ffn_conv_b, ffn_down=ffn_down,
             ln_ffn_g=ln_ffn_g, ln_ffn_b=ln_ffn_b, ple_proj=ple_proj, ple_gate_w=ple_gate_w,
             ple_gate_b=ple_gate_b)
    bp, seq, d = x_prompt.shape
    bs, dseq, _ = x_sample.shape
    f32 = x_prompt.dtype
    conv_dim = ssm_conv_w.shape[-1]
    z_pool = jnp.zeros((1, bp, POOL_HIST, d), f32)
    z_sconv = jnp.zeros((1, bp, SSM_CONV - 1, conv_dim), f32)
    z_state = jnp.zeros((1, bp, N_SSM_HEADS, HEAD_DIM, D_STATE), f32)
    z_ffn = jnp.zeros((DEPTH, bp, FFN_CONV - 1, ffn_conv_w.shape[-1]), f32)
    tl_p = _block_len(seq, 512)
    tl_s = _block_len(seq, 256)
    y_p, pool_p, sconv_p, state_p, ffn_p = _trunk(
        x_prompt, p_prompt, z_pool, z_sconv, z_state, z_ffn, w, pos0=0, q=min(SSD_BLOCK, seq),
        pool_blk=(1, tl_p), ssd_blk=(1, tl_s), ffn_blk=(1, tl_p))
    y_s, pool_s, sconv_s, state_s, ffn_s = _trunk(
        x_sample, p_sample, cache_pool, cache_ssm_conv, state_ssm, cache_ffn_conv, w, pos0=PAST_LEN,
        q=dseq, pool_blk=(bs, dseq), ssd_blk=(2, dseq), ffn_blk=(bs, dseq))
    return (y_p, y_s, pool_p, pool_s, sconv_p, sconv_s, state_p, state_s, ffn_p, ffn_s)
```

```python
import functools

import jax
import jax.numpy as jnp
from jax import lax
from jax.experimental import pallas as pl
from jax.experimental.pallas import tpu as pltpu

PAST_LEN = 1024
DEPTH = 2
ALPHA = (2 * DEPTH) ** 0.25
LN_EPS = 1e-5
RMS_EPS = 1e-5
POOL_WINDOWS = (2, 4, 8, 16)
POOL_HIST = max(POOL_WINDOWS) - 1
HEAD_DIM = 64
N_SSM_GROUPS = 8
HEADS_PER_GROUP = 4
N_SSM_HEADS = N_SSM_GROUPS * HEADS_PER_GROUP
D_STATE = 128
SSM_CONV = 4
FFN_CONV = 3
SSD_BLOCK = 64

LANES = 128
SUBLANES = 8
VMEM_LIMIT_BYTES = 56 * 1024 * 1024

HIST_ROWS = 2 * SUBLANES
GROUP_W = HEADS_PER_GROUP * HEAD_DIM


def _bdot(a, b):
    return jnp.dot(a.astype(jnp.bfloat16), b.astype(jnp.bfloat16), preferred_element_type=jnp.float32)


def _sigmoid(v):
    return 1.0 / (1.0 + jnp.exp(-v))


def _silu(v):
    return v * _sigmoid(v)


def _layer_norm(v, g, b):
    mu = jnp.mean(v, axis=-1, keepdims=True)
    vc = v - mu
    var = jnp.mean(vc * vc, axis=-1, keepdims=True)
    return vc * lax.rsqrt(var + LN_EPS) * g + b


def _const_spec(shape):
    nd = len(shape)
    return pl.BlockSpec(shape, lambda b, l: (0,) * nd, pipeline_mode=pl.Buffered(1))


def _split3(v):
    hi = v.astype(jnp.bfloat16).astype(jnp.float32)
    r1 = v - hi
    mid = r1.astype(jnp.bfloat16).astype(jnp.float32)
    lo = (r1 - mid).astype(jnp.bfloat16).astype(jnp.float32)
    return hi, mid, lo


def _phase_rows(ref, ph):
    c = ref.shape[-1] // ph
    return jnp.concatenate([ref[:, :, a * c:(a + 1) * c] for a in range(ph)], axis=1)


def _phase_conv(work, hist, cw_ref, cb_ref, cols, taps):
    ph = work.shape[1]
    r = work.shape[2] - SUBLANES
    for i in range(taps - 1):
        a = ph - (taps - 1) + i
        rows = slice(i * SUBLANES, (i + 1) * SUBLANES)
        work[:, a, 0:SUBLANES, :] = hist[:, rows, :]
        hist[:, rows, :] = work[:, a, r:r + SUBLANES, :]
    outs = []
    for a in range(ph):
        c = cb_ref[:, cols]
        for k in range(taps):
            dist = taps - 1 - k
            if a >= dist:
                src = work[:, a - dist, SUBLANES:SUBLANES + r, :]
            else:
                src = work[:, a - dist + ph, SUBLANES - 1:SUBLANES - 1 + r, :]
            c = c + cw_ref[k:k + 1, cols] * src
        outs.append(c)
    return outs


def _pool_kernel(x_ref, hist_ref, w_ref, scale_ref, g_ref, b_ref, out_ref, newhist_ref, xh_ref,
                 *, nb, tl, d, pos0):
    l = pl.program_id(1)
    nl = pl.num_programs(1)
    gw = d // len(POOL_WINDOWS)

    @pl.when(l == 0)
    def _():
        xh_ref[:, 0:HIST_ROWS, :] = hist_ref[...]

    x = x_ref[...]
    xh_ref[:, HIST_ROWS:HIST_ROWS + tl, :] = x

    pos = pos0 + l * tl + lax.broadcasted_iota(jnp.int32, (1, tl, gw), 1)
    ys = []
    for gi, wsz in enumerate(POOL_WINDOWS):
        cols = slice(gi * gw, (gi + 1) * gw)
        s = x[:, :, cols]
        for k in range(1, wsz):
            s = s + xh_ref[:, HIST_ROWS - k:HIST_ROWS - k + tl, cols]
        cnt = jnp.minimum(wsz, pos + 1).astype(jnp.float32)
        pooled = (s / cnt - x[:, :, cols]).reshape(nb * tl, gw)
        ys.append(_bdot(pooled, w_ref[gi]))
    y = jnp.concatenate(ys, axis=-1) * scale_ref[...]
    xf = x.reshape(nb * tl, d)
    out = _layer_norm(ALPHA * xf + y, g_ref[...], b_ref[...])
    out_ref[...] = out.reshape(nb, tl, d)

    @pl.when(l == nl - 1)
    def _():
        newhist_ref[...] = xh_ref[:, HIST_ROWS + tl - POOL_HIST:HIST_ROWS + tl, :]

    xh_ref[:, 0:HIST_ROWS, :] = xh_ref[:, tl:tl + HIST_ROWS, :]


def _pool_layer(x, hist, w, scale, g, b, *, pos0, nb, tl):
    bt, seq, d = x.shape
    assert bt % nb == 0 and seq % tl == 0 and tl % HIST_ROWS == 0
    hist16 = jnp.pad(hist, ((0, 0), (HIST_ROWS - POOL_HIST, 0), (0, 0)))
    kern = functools.partial(_pool_kernel, nb=nb, tl=tl, d=d, pos0=pos0)
    return pl.pallas_call(
        kern,
        grid=(bt // nb, seq // tl),
        in_specs=[
            pl.BlockSpec((nb, tl, d), lambda b, l: (b, l, 0)),
            pl.BlockSpec((nb, HIST_ROWS, d), lambda b, l: (b, 0, 0)),
            _const_spec(w.shape),
            _const_spec((1, d)), _const_spec((1, d)), _const_spec((1, d)),
        ],
        out_specs=[
            pl.BlockSpec((nb, tl, d), lambda b, l: (b, l, 0)),
            pl.BlockSpec((nb, POOL_HIST, d), lambda b, l: (b, 0, 0)),
        ],
        out_shape=[
            jax.ShapeDtypeStruct((bt, seq, d), jnp.float32),
            jax.ShapeDtypeStruct((bt, POOL_HIST, d), jnp.float32),
        ],
        scratch_shapes=[pltpu.VMEM((nb, HIST_ROWS + tl, d), jnp.float32)],
        compiler_params=pltpu.CompilerParams(
            dimension_semantics=("arbitrary", "arbitrary"), vmem_limit_bytes=VMEM_LIMIT_BYTES),
        name="pool_mixer",
    )(x, hist16, w.astype(jnp.bfloat16), scale.reshape(1, d), g.reshape(1, d), b.reshape(1, d))


def _ffn_kernel(x_ref, p_ref, hist_ref, wup_ref, cw_ref, cb_ref, wdown_ref, g_ref, b_ref,
                wg_ref, bg_ref, wp_ref, out_ref, newhist_ref, hist_s, work_s, acc_s,
                *, nb, tl, d, dff, fc, ph):
    l = pl.program_id(1)
    nl = pl.num_programs(1)
    nc = dff // fc
    t = nb * tl
    r = tl // ph

    @pl.when(l == 0)
    def _():
        for j in range(2 * nc):
            hist_s[j] = hist_ref[:, :, j * fc:(j + 1) * fc]

    x = _phase_rows(x_ref, ph).reshape(t, d)
    xb = x.astype(jnp.bfloat16)

    for j in range(nc):
        halves = []
        for half in range(2):
            cols = slice(half * dff + j * fc, half * dff + (j + 1) * fc)
            work = work_s.at[2 * (j % 2) + half]
            work[:, :, SUBLANES:, :] = _bdot(xb, wup_ref[half, j]).reshape(nb, ph, r, fc)
            outs = _phase_conv(work, hist_s.at[half * nc + j], cw_ref, cb_ref, cols, FFN_CONV)
            halves.append(jnp.concatenate(outs, axis=1).reshape(t, fc))
        part = _bdot(_silu(halves[0]) * halves[1], wdown_ref[j])
        if j == 0:
            acc_s[...] = part
        else:
            acc_s[...] += part

    x2 = _layer_norm(ALPHA * x + acc_s[...], g_ref[...], b_ref[...])
    gate = _sigmoid(_bdot(x2, wg_ref[...]) + bg_ref[...])
    pp = _bdot(_phase_rows(p_ref, ph).reshape(t, p_ref.shape[-1] // ph), wp_ref[...])
    res = (x2 + gate * pp).reshape(nb, ph, r, d)
    for a in range(ph):
        out_ref[:, :, a * d:(a + 1) * d] = res[:, a]

    @pl.when(l == nl - 1)
    def _():
        for j in range(2 * nc):
            for i in range(FFN_CONV - 1):
                row = (i + 1) * SUBLANES - 1
                newhist_ref[:, i:i + 1, j * fc:(j + 1) * fc] = hist_s[j, :, row:row + 1, :]


def _ffn_layer(x, p, hist, w_up, conv_w, conv_b, w_down, g, b, gate_w, gate_b, ple_proj, *, nb, tl, ph):
    bt, seq, d = x.shape
    pd = p.shape[-1]
    dff = w_down.shape[0]
    fc = 2 * LANES
    front = (FFN_CONV - 1) * SUBLANES
    r = tl // ph
    assert bt % nb == 0 and seq % tl == 0 and tl % ph == 0 and r % SUBLANES == 0 and dff % fc == 0
    assert ph >= FFN_CONV - 1
    nc = dff // fc
    hist_rep = jnp.repeat(hist, SUBLANES, axis=1)
    wup = w_up.astype(jnp.bfloat16).reshape(d, 2, nc, fc).transpose(1, 2, 0, 3)
    wdown = w_down.astype(jnp.bfloat16).reshape(nc, fc, d)
    kern = functools.partial(_ffn_kernel, nb=nb, tl=tl, d=d, dff=dff, fc=fc, ph=ph)
    out, newhist = pl.pallas_call(
        kern,
        grid=(bt // nb, seq // tl),
        in_specs=[
            pl.BlockSpec((nb, r, ph * d), lambda b, l: (b, l, 0)),
            pl.BlockSpec((nb, r, ph * pd), lambda b, l: (b, l, 0)),
            pl.BlockSpec((nb, front, 2 * dff), lambda b, l: (b, 0, 0)),
            _const_spec(wup.shape),
            _const_spec((FFN_CONV, 2 * dff)), _const_spec((1, 2 * dff)),
            _const_spec(wdown.shape),
            _const_spec((1, d)), _const_spec((1, d)),
            _const_spec((d, d)), _const_spec((1, d)), _const_spec((pd, d)),
        ],
        out_specs=[
            pl.BlockSpec((nb, r, ph * d), lambda b, l: (b, l, 0)),
            pl.BlockSpec((nb, FFN_CONV - 1, 2 * dff), lambda b, l: (b, 0, 0)),
        ],
        out_shape=[
            jax.ShapeDtypeStruct((bt, seq // ph, ph * d), jnp.float32),
            jax.ShapeDtypeStruct((bt, FFN_CONV - 1, 2 * dff), jnp.float32),
        ],
        scratch_shapes=[
            pltpu.VMEM((2 * nc, nb, front, fc), jnp.float32),
            pltpu.VMEM((4, nb, ph, SUBLANES + r, fc), jnp.float32),
            pltpu.VMEM((nb * tl, d), jnp.float32),
        ],
        compiler_params=pltpu.CompilerParams(
            dimension_semantics=("arbitrary", "arbitrary"), vmem_limit_bytes=VMEM_LIMIT_BYTES),
        name="conv_ffn",
    )(x.reshape(bt, seq // ph, ph * d), p.reshape(bt, seq // ph, ph * pd), hist_rep, wup, conv_w,
      conv_b.reshape(1, 2 * dff), wdown, g.reshape(1, d), b.reshape(1, d),
      gate_w.astype(jnp.bfloat16), gate_b.reshape(1, d), ple_proj.astype(jnp.bfloat16))
    return out.reshape(bt, seq, d), newhist


def _ssd_kernel(x_ref, hist_ref, s0_ref, wz_ref, wxbc_ref, wdt_ref, cw_ref, cb_ref, dtb_ref, alog_ref,
                dskip_ref, nw_ref, wout_ref, g_ref, b_ref, tri_ref, eq_ref, ep_ref,
                out_ref, newhist_ref, state_ref,
                hist_s, work_s, xs_s, b_s, c_s, dt_s, acs_s, cfq_s, cfp_s, dtp_s, y_s, yb_s,
                *, nb, tl, d, q):
    l = pl.program_id(1)
    nl = pl.num_programs(1)
    t = nb * tl
    d_inner = N_SSM_HEADS * HEAD_DIM
    gn = N_SSM_GROUPS * D_STATE
    conv_dim = d_inner + 2 * gn
    ph = q // SUBLANES
    cpb = tl // q
    qw = HEADS_PER_GROUP * q
    cc = 4 * LANES
    bf16 = jnp.bfloat16

    @pl.when(l == 0)
    def _():
        hist_s[...] = hist_ref[...]
        state_ref[...] = s0_ref[...]

    tiles = [(n, c, a) for n in range(nb) for c in range(cpb) for a in range(ph)]

    def tile_rows(k):
        return slice(k * SUBLANES, (k + 1) * SUBLANES)

    x = jnp.concatenate(
        [x_ref[n, c * SUBLANES:(c + 1) * SUBLANES, a * d:(a + 1) * d] for n, c, a in tiles], axis=0)
    xb = x.astype(bf16)

    for jc in range(conv_dim // cc):
        cols = slice(jc * cc, (jc + 1) * cc)
        work = work_s.at[jc % 2]
        xbc = _bdot(xb, wxbc_ref[:, cols])
        for k, (n, c, a) in enumerate(tiles):
            work[n, a, (c + 1) * SUBLANES:(c + 2) * SUBLANES, :] = xbc[tile_rows(k), :]
        outs = _phase_conv(work, hist_s.at[jc], cw_ref, cb_ref, cols, SSM_CONV)
        outs = [_silu(v) for v in outs]
        if (jc + 1) * cc <= d_inner:
            dest, off = xs_s, jc * cc
        elif (jc + 1) * cc <= d_inner + gn:
            dest, off = b_s, jc * cc - d_inner
        else:
            dest, off = c_s, jc * cc - d_inner - gn
        for k, (n, c, a) in enumerate(tiles):
            dest[tile_rows(k), off:off + cc] = outs[a][n, c * SUBLANES:(c + 1) * SUBLANES, :]

    dt = _bdot(xb, wdt_ref[...]) + dtb_ref[...]
    dt = jnp.maximum(dt, 0.0) + jnp.log1p(jnp.exp(-jnp.abs(dt)))
    dt_s[...] = dt
    a_row = -jnp.exp(alog_ref[...])
    lane = lax.broadcasted_iota(jnp.int32, (q, LANES), 1)

    def lane_split(v):
        hi, mid, lo_ = _split3(v)
        return jnp.where(lane < N_SSM_HEADS, hi, jnp.where(lane < 2 * N_SSM_HEADS, mid, lo_))

    for i in range(nb * cpb):
        rows = slice(i * q, (i + 1) * q)
        a = dt_s[rows, :] * a_row
        hi, mid, lo_ = _split3(a)
        stacked = jnp.concatenate([hi, mid, lo_, jnp.zeros_like(hi)], axis=0).astype(bf16)
        acs = jnp.dot(tri_ref[...], stacked, preferred_element_type=jnp.float32)
        acs_s[rows, :] = lane_split(acs)
        dt_s[rows, :] = lane_split(dt_s[rows, :])
    acs3 = acs_s[...].astype(bf16)
    cfq_s[...] = jnp.dot(acs3, eq_ref[...], preferred_element_type=jnp.float32)
    if q != HEAD_DIM:
        cfp_s[...] = jnp.dot(acs3, ep_ref[...], preferred_element_type=jnp.float32)
    dtp_s[...] = jnp.dot(dt_s[...].astype(bf16), ep_ref[...], preferred_element_type=jnp.float32)
    cfp = cfq_s if q == HEAD_DIM else cfp_s

    def chunk_time(idx):
        return ph * (idx & (SUBLANES - 1)) + (idx >> 3)

    row_q = lax.broadcasted_iota(jnp.int32, (q, qw), 0)
    lane_q = lax.broadcasted_iota(jnp.int32, (q, qw), 1) & (q - 1)
    eye_t = row_q == lane_q
    causal = chunk_time(lane_q) <= chunk_time(row_q)
    bd_mask = ((lax.broadcasted_iota(jnp.int32, (qw, GROUP_W), 0) >> (q.bit_length() - 1))
               == (lax.broadcasted_iota(jnp.int32, (qw, GROUP_W), 1) >> (HEAD_DIM.bit_length() - 1)))
    pad_rows = LANES - q

    def chunk_body(i, carry):
        n = i // cpb
        rows = pl.ds(pl.multiple_of(i * q, q), q)
        for g in range(N_SSM_GROUPS):
            ncols = slice(g * D_STATE, (g + 1) * D_STATE)
            pcols = slice(g * GROUP_W, (g + 1) * GROUP_W)
            bg = b_s[rows, ncols]
            cg = c_s[rows, ncols].astype(bf16)
            cf = cfq_s[rows, g * qw:(g + 1) * qw]
            cfg = cfp[rows, pcols]
            xdt = xs_s[rows, pcols] * dtp_s[rows, pcols]
            rowf = jnp.sum(jnp.where(eye_t, cf, 0.0), axis=0, keepdims=True)
            lm = jnp.where(causal, jnp.exp(cf - rowf), 0.0)
            b4 = jnp.concatenate([bg.astype(bf16)] * HEADS_PER_GROUP, axis=0)
            cb4 = lax.dot_general(cg, b4, (((1,), (1,)), ((), ())), preferred_element_type=jnp.float32)
            m = (cb4 * lm).astype(bf16)
            xbd = jnp.where(bd_mask, jnp.concatenate([xdt] * HEADS_PER_GROUP, axis=0), 0.0).astype(bf16)
            y_intra = jnp.dot(m, xbd, preferred_element_type=jnp.float32)
            st = state_ref[n, g]
            y_inter = jnp.dot(cg, st.astype(bf16), preferred_element_type=jnp.float32) * jnp.exp(cfg)
            y_s[rows, pcols] = y_intra + y_inter
            last = cfg[q - 1:q, :]
            xd = xdt * jnp.exp(last - cfg)
            if pad_rows:
                bgp = jnp.concatenate([bg, jnp.zeros((pad_rows, D_STATE), jnp.float32)], axis=0)
                xd = jnp.concatenate([xd, jnp.zeros((pad_rows, GROUP_W), jnp.float32)], axis=0)
            else:
                bgp = bg
            upd = jnp.dot(bgp.T.astype(bf16), xd.astype(bf16), preferred_element_type=jnp.float32)
            state_ref[n, g] = st * jnp.exp(last) + upd
        return carry

    lax.fori_loop(0, nb * cpb, chunk_body, 0)

    for g in range(N_SSM_GROUPS):
        pcols = slice(g * GROUP_W, (g + 1) * GROUP_W)
        yg = y_s[:, pcols] + dskip_ref[:, pcols] * xs_s[:, pcols]
        yg = yg * _silu(_bdot(xb, wz_ref[:, pcols]))
        yg = yg * lax.rsqrt(jnp.mean(yg * yg, axis=-1, keepdims=True) + RMS_EPS)
        yb_s[:, pcols] = (yg * nw_ref[:, pcols]).astype(bf16)
    mix = jnp.dot(yb_s[...], wout_ref[...], preferred_element_type=jnp.float32)
    res = _layer_norm(ALPHA * x + mix, g_ref[...], b_ref[...])
    for k, (n, c, a) in enumerate(tiles):
        out_ref[n, c * SUBLANES:(c + 1) * SUBLANES, a * d:(a + 1) * d] = res[tile_rows(k), :]

    @pl.when(l == nl - 1)
    def _():
        for jc in range(conv_dim // cc):
            for i in range(SSM_CONV - 1):
                row = (i + 1) * SUBLANES - 1
                newhist_ref[:, i:i + 1, jc * cc:(jc + 1) * cc] = hist_s[jc, :, row:row + 1, :]


def _expand_matrix(rep):
    k = lax.broadcasted_iota(jnp.int32, (LANES, N_SSM_HEADS * rep), 0)
    j = lax.broadcasted_iota(jnp.int32, (LANES, N_SSM_HEADS * rep), 1)
    return ((k < 3 * N_SSM_HEADS) & (k % N_SSM_HEADS == j // rep)).astype(jnp.bfloat16)


def _ssd_layer(x, conv_hist, state, in_proj, conv_w, conv_b, dt_bias, a_log, d_skip, norm_w, out_proj,
               g, b, *, q, nb, tl):
    bt, seq, d = x.shape
    d_inner = N_SSM_HEADS * HEAD_DIM
    conv_dim = d_inner + 2 * N_SSM_GROUPS * D_STATE
    assert bt % nb == 0 and seq % tl == 0 and tl % q == 0 and q % SUBLANES == 0 and q <= LANES
    assert q & (q - 1) == 0, "chunk length must be a power of two"
    t = nb * tl
    qw = HEADS_PER_GROUP * q
    cc = 4 * LANES
    front = (SSM_CONV - 1) * SUBLANES
    bf16 = jnp.bfloat16

    wz = in_proj[:, :d_inner].astype(bf16)
    wxbc = in_proj[:, d_inner:d_inner + conv_dim].astype(bf16)
    wdt = in_proj[:, d_inner + conv_dim:].astype(bf16)
    pad = LANES - 3 * N_SSM_HEADS
    rep3 = lambda v: jnp.pad(jnp.concatenate([v] * 3, axis=-1), ((0, 0), (0, pad)))
    wdt3 = rep3(wdt)
    dtb3 = rep3(dt_bias.reshape(1, N_SSM_HEADS))
    alog3 = rep3(a_log.reshape(1, N_SSM_HEADS))
    dskip = jnp.repeat(d_skip, HEAD_DIM).reshape(1, d_inner)
    hist_rep = jnp.repeat(conv_hist, SUBLANES, axis=1).reshape(bt, front, conv_dim // cc, cc)
    hist_rep = hist_rep.transpose(2, 0, 1, 3)
    s0 = state.reshape(bt, N_SSM_GROUPS, HEADS_PER_GROUP, HEAD_DIM, D_STATE)
    s0 = s0.transpose(0, 1, 4, 2, 3).reshape(bt, N_SSM_GROUPS, D_STATE, GROUP_W)
    ph = q // SUBLANES
    r = tl // ph
    chunk_time = lambda idx: ph * (idx % SUBLANES) + idx // SUBLANES
    tri_col = lax.broadcasted_iota(jnp.int32, (q, 4 * q), 1)
    tri = chunk_time(tri_col % q) <= chunk_time(lax.broadcasted_iota(jnp.int32, (q, 4 * q), 0))
    tri = (tri & (tri_col < 3 * q)).astype(bf16)
    eq = _expand_matrix(q)
    ep = _expand_matrix(HEAD_DIM)

    kern = functools.partial(_ssd_kernel, nb=nb, tl=tl, d=d, q=q)
    f32 = jnp.float32
    out, newhist, newstate = pl.pallas_call(
        kern,
        grid=(bt // nb, seq // tl),
        in_specs=[
            pl.BlockSpec((nb, r, ph * d), lambda b, l: (b, l, 0)),
            pl.BlockSpec((conv_dim // cc, nb, front, cc), lambda b, l: (0, b, 0, 0)),
            pl.BlockSpec((nb, N_SSM_GROUPS, D_STATE, GROUP_W), lambda b, l: (b, 0, 0, 0)),
            _const_spec(wz.shape), _const_spec(wxbc.shape), _const_spec(wdt3.shape),
            _const_spec((SSM_CONV, conv_dim)), _const_spec((1, conv_dim)),
            _const_spec((1, LANES)), _const_spec((1, LANES)),
            _const_spec((1, d_inner)), _const_spec((1, d_inner)),
            _const_spec((d_inner, d)), _const_spec((1, d)), _const_spec((1, d)),
            _const_spec(tri.shape), _const_spec(eq.shape), _const_spec(ep.shape),
        ],
        out_specs=[
            pl.BlockSpec((nb, r, ph * d), lambda b, l: (b, l, 0)),
            pl.BlockSpec((nb, SSM_CONV - 1, conv_dim), lambda b, l: (b, 0, 0)),
            pl.BlockSpec((nb, N_SSM_GROUPS, D_STATE, GROUP_W), lambda b, l: (b, 0, 0, 0)),
        ],
        out_shape=[
            jax.ShapeDtypeStruct((bt, seq // ph, ph * d), f32),
            jax.ShapeDtypeStruct((bt, SSM_CONV - 1, conv_dim), f32),
            jax.ShapeDtypeStruct((bt, N_SSM_GROUPS, D_STATE, GROUP_W), f32),
        ],
        scratch_shapes=[
            pltpu.VMEM((conv_dim // cc, nb, front, cc), f32),
            pltpu.VMEM((2, nb, ph, SUBLANES + r, cc), f32),
            pltpu.VMEM((t, d_inner), f32),
            pltpu.VMEM((t, N_SSM_GROUPS * D_STATE), f32),
            pltpu.VMEM((t, N_SSM_GROUPS * D_STATE), f32),
            pltpu.VMEM((t, LANES), f32),
            pltpu.VMEM((t, LANES), f32),
            pltpu.VMEM((t, N_SSM_GROUPS * qw), f32),
            pltpu.VMEM((t, d_inner) if q != HEAD_DIM else (SUBLANES, LANES), f32),
            pltpu.VMEM((t, d_inner), f32),
            pltpu.VMEM((t, d_inner), f32),
            pltpu.VMEM((t, d_inner), bf16),
        ],
        compiler_params=pltpu.CompilerParams(
            dimension_semantics=("arbitrary", "arbitrary"), vmem_limit_bytes=VMEM_LIMIT_BYTES),
        name="ssd_mixer",
    )(x.reshape(bt, seq // ph, ph * d), hist_rep, s0, wz, wxbc, wdt3, conv_w, conv_b.reshape(1, conv_dim),
      dtb3, alog3, dskip, norm_w.reshape(1, d_inner), out_proj.astype(bf16), g.reshape(1, d),
      b.reshape(1, d), tri, eq, ep)
    newstate = newstate.reshape(bt, N_SSM_GROUPS, D_STATE, HEADS_PER_GROUP, HEAD_DIM)
    newstate = newstate.transpose(0, 1, 3, 4, 2).reshape(bt, N_SSM_HEADS, HEAD_DIM, D_STATE)
    return out.reshape(bt, seq, d), newhist, newstate


def _trunk(x, p, pool_hist, ssm_conv_hist, ssm_state, ffn_hist, w, *, pos0, q, pool_blk, ssd_blk, ffn_blk):
    x, new_pool = _pool_layer(x, pool_hist[0], w['pool_w'][0], w['pool_scale'][0],
                              w['ln_mix_g'][0], w['ln_mix_b'][0], pos0=pos0, nb=pool_blk[0], tl=pool_blk[1])
    new_ffn = []
    x, fh = _ffn_layer(x, p[0], ffn_hist[0], w['ffn_up'][0], w['ffn_conv_w'][0], w['ffn_conv_b'][0],
                       w['ffn_down'][0], w['ln_ffn_g'][0], w['ln_ffn_b'][0], w['ple_gate_w'][0],
                       w['ple_gate_b'][0], w['ple_proj'][0], nb=ffn_blk[0], tl=ffn_blk[1], ph=ffn_blk[2])
    new_ffn.append(fh)
    x, new_conv, new_state = _ssd_layer(
        x, ssm_conv_hist[0], ssm_state[0], w['ssm_in_proj'][0], w['ssm_conv_w'][0], w['ssm_conv_b'][0],
        w['ssm_dt_bias'][0], w['ssm_A_log'][0], w['ssm_D'][0], w['ssm_norm_w'][0], w['ssm_out_proj'][0],
        w['ln_mix_g'][1], w['ln_mix_b'][1], q=q, nb=ssd_blk[0], tl=ssd_blk[1])
    x, fh = _ffn_layer(x, p[1], ffn_hist[1], w['ffn_up'][1], w['ffn_conv_w'][1], w['ffn_conv_b'][1],
                       w['ffn_down'][1], w['ln_ffn_g'][1], w['ln_ffn_b'][1], w['ple_gate_w'][1],
                       w['ple_gate_b'][1], w['ple_proj'][1], nb=ffn_blk[0], tl=ffn_blk[1], ph=ffn_blk[2])
    new_ffn.append(fh)
    return x, new_pool[None], new_conv[None], new_state[None], jnp.stack(new_ffn)


def _phases(tl):
    ph = SUBLANES
    while tl % (ph * SUBLANES):
        ph //= 2
    return ph


def _block_len(seq, target):
    tl = min(seq, target)
    while seq % tl:
        tl //= 2
    return tl


def kernel(x_prompt, x_sample, p_prompt, p_sample, cache_pool, cache_ssm_conv, state_ssm, cache_ffn_conv,
           pool_w, pool_scale, ssm_in_proj, ssm_conv_w, ssm_conv_b, ssm_dt_bias, ssm_A_log, ssm_D, ssm_norm_w,
           ssm_out_proj, ln_mix_g, ln_mix_b, ffn_up, ffn_conv_w, ffn_conv_b, ffn_down, ln_ffn_g, ln_ffn_b,
           ple_proj, ple_gate_w, ple_gate_b):
    assert pool_w.shape[0] == 1 and ssm_in_proj.shape[0] == 1 and ffn_up.shape[0] == DEPTH
    w = dict(pool_w=pool_w, pool_scale=pool_scale, ssm_in_proj=ssm_in_proj, ssm_conv_w=ssm_conv_w,
             ssm_conv_b=ssm_conv_b, ssm_dt_bias=ssm_dt_bias, ssm_A_log=ssm_A_log, ssm_D=ssm_D,
             ssm_norm_w=ssm_norm_w, ssm_out_proj=ssm_out_proj, ln_mix_g=ln_mix_g, ln_mix_b=ln_mix_b,
             ffn_up=ffn_up, ffn_conv_w=ffn_conv_w, ffn_conv_b=ffn_conv_b, ffn_down=ffn_down,
             ln_ffn_g=ln_ffn_g, ln_ffn_b=ln_ffn_b, ple_proj=ple_proj, ple_gate_w=ple_gate_w,
             ple_gate_b=ple_gate_b)
    bp, seq, d = x_prompt.shape
    bs, dseq, _ = x_sample.shape
    f32 = x_prompt.dtype
    conv_dim = ssm_conv_w.shape[-1]
    z_pool = jnp.zeros((1, bp, POOL_HIST, d), f32)
    z_sconv = jnp.zeros((1, bp, SSM_CONV - 1, conv_dim), f32)
    z_state = jnp.zeros((1, bp, N_SSM_HEADS, HEAD_DIM, D_STATE), f32)
    z_ffn = jnp.zeros((DEPTH, bp, FFN_CONV - 1, ffn_conv_w.shape[-1]), f32)
    tl_p = _block_len(seq, 512)
    tl_s = _block_len(seq, 256)
    y_p, pool_p, sconv_p, state_p, ffn_p = _trunk(
        x_prompt, p_prompt, z_pool, z_sconv, z_state, z_ffn, w, pos0=0, q=min(SSD_BLOCK, seq),
        pool_blk=(1, tl_p), ssd_blk=(1, tl_s), ffn_blk=(1, tl_p, _phases(tl_p)))
    y_s, pool_s, sconv_s, state_s, ffn_s = _trunk(
        x_sample, p_sample, cache_pool, cache_ssm_conv, state_ssm, cache_ffn_conv, w, pos0=PAST_LEN,
        q=dseq, pool_blk=(bs, dseq), ssd_blk=(2, dseq), ffn_blk=(bs, dseq, _phases(dseq)))
    return (y_p, y_s, pool_p, pool_s, sconv_p, sconv_s, state_p, state_s, ffn_p, ffn_s)
```

```python
import functools

import jax
import jax.numpy as jnp
from jax import lax
from jax.experimental import pallas as pl
from jax.experimental.pallas import tpu as pltpu

PAST_LEN = 1024
DEPTH = 2
ALPHA = (2 * DEPTH) ** 0.25
LN_EPS = 1e-5
RMS_EPS = 1e-5
POOL_WINDOWS = (2, 4, 8, 16)
POOL_HIST = max(POOL_WINDOWS) - 1
HEAD_DIM = 64
N_SSM_GROUPS = 8
HEADS_PER_GROUP = 4
N_SSM_HEADS = N_SSM_GROUPS * HEADS_PER_GROUP
D_STATE = 128
SSM_CONV = 4
FFN_CONV = 3
SSD_BLOCK = 64

LANES = 128
SUBLANES = 8
VMEM_LIMIT_BYTES = 56 * 1024 * 1024

HIST_ROWS = 2 * SUBLANES
GROUP_W = HEADS_PER_GROUP * HEAD_DIM


def _bdot(a, b):
    return jnp.dot(a.astype(jnp.bfloat16), b.astype(jnp.bfloat16), preferred_element_type=jnp.float32)


def _sigmoid(v):
    return 1.0 / (1.0 + jnp.exp(-v))


def _silu(v):
    return v * _sigmoid(v)


def _layer_norm(v, g, b):
    mu = jnp.mean(v, axis=-1, keepdims=True)
    vc = v - mu
    var = jnp.mean(vc * vc, axis=-1, keepdims=True)
    return vc * lax.rsqrt(var + LN_EPS) * g + b


def _const_spec(shape):
    nd = len(shape)
    return pl.BlockSpec(shape, lambda b, l: (0,) * nd, pipeline_mode=pl.Buffered(1))


def _split3(v):
    hi = v.astype(jnp.bfloat16).astype(jnp.float32)
    r1 = v - hi
    mid = r1.astype(jnp.bfloat16).astype(jnp.float32)
    lo = (r1 - mid).astype(jnp.bfloat16).astype(jnp.float32)
    return hi, mid, lo


def _phase_rows(ref, ph):
    c = ref.shape[-1] // ph
    return jnp.concatenate([ref[:, :, a * c:(a + 1) * c] for a in range(ph)], axis=1)


def _store_phases(out_ref, slab_ref, v, nb, tl, ph):
    d = v.shape[-1]
    r = tl // ph
    for k in range(d // LANES):
        slab_ref[k] = v[:, k * LANES:(k + 1) * LANES]
    for n in range(nb):
        for a in range(ph):
            for k in range(d // LANES):
                out_ref[n, :, a * d + k * LANES:a * d + (k + 1) * LANES] = (
                    slab_ref[k, pl.ds(n * tl + a, r, stride=ph), :])


def _store_time_order(out_ref, slab_ref, v, nb, tl, ph):
    d = v.shape[-1]
    r = tl // ph
    for n in range(nb):
        for a in range(ph):
            for k in range(d // LANES):
                slab_ref[k, pl.ds(n * tl + a, r, stride=ph), :] = v[n, a, :, k * LANES:(k + 1) * LANES]
    for k in range(d // LANES):
        out_ref[:, :, k * LANES:(k + 1) * LANES] = slab_ref[k].reshape(nb, tl, LANES)


def _phase_rows_strided(tile_refs, tl, ph):
    r = tl // ph
    return jnp.concatenate(
        [jnp.concatenate([ref[:, pl.ds(a, r, stride=ph), :] for ref in tile_refs], axis=-1)
         for a in range(ph)], axis=1)


def _phase_conv(work, hist, cw_ref, cb_ref, cols, taps):
    ph = work.shape[1]
    r = work.shape[2] - SUBLANES
    for i in range(taps - 1):
        a = ph - (taps - 1) + i
        rows = slice(i * SUBLANES, (i + 1) * SUBLANES)
        work[:, a, 0:SUBLANES, :] = hist[:, rows, :]
        hist[:, rows, :] = work[:, a, r:r + SUBLANES, :]
    outs = []
    for a in range(ph):
        c = cb_ref[:, cols]
        for k in range(taps):
            dist = taps - 1 - k
            if a >= dist:
                src = work[:, a - dist, SUBLANES:SUBLANES + r, :]
            else:
                src = work[:, a - dist + ph, SUBLANES - 1:SUBLANES - 1 + r, :]
            c = c + cw_ref[k:k + 1, cols] * src
        outs.append(c)
    return outs


def _pool_kernel(x_ref, hist_ref, w_ref, scale_ref, g_ref, b_ref, out_ref, newhist_ref, xh_ref, slab_ref,
                 *, nb, tl, d, pos0, ph):
    l = pl.program_id(1)
    nl = pl.num_programs(1)
    gw = d // len(POOL_WINDOWS)

    @pl.when(l == 0)
    def _():
        xh_ref[:, 0:HIST_ROWS, :] = hist_ref[...]

    x = x_ref[...]
    xh_ref[:, HIST_ROWS:HIST_ROWS + tl, :] = x

    pos = pos0 + l * tl + lax.broadcasted_iota(jnp.int32, (1, tl, gw), 1)
    ys = []
    for gi, wsz in enumerate(POOL_WINDOWS):
        cols = slice(gi * gw, (gi + 1) * gw)
        s = x[:, :, cols]
        for k in range(1, wsz):
            s = s + xh_ref[:, HIST_ROWS - k:HIST_ROWS - k + tl, cols]
        cnt = jnp.minimum(wsz, pos + 1).astype(jnp.float32)
        pooled = (s / cnt - x[:, :, cols]).reshape(nb * tl, gw)
        ys.append(_bdot(pooled, w_ref[gi]))
    y = jnp.concatenate(ys, axis=-1) * scale_ref[...]
    xf = x.reshape(nb * tl, d)
    out = _layer_norm(ALPHA * xf + y, g_ref[...], b_ref[...])
    _store_phases(out_ref, slab_ref, out, nb, tl, ph)

    @pl.when(l == nl - 1)
    def _():
        newhist_ref[...] = xh_ref[:, HIST_ROWS + tl - POOL_HIST:HIST_ROWS + tl, :]

    xh_ref[:, 0:HIST_ROWS, :] = xh_ref[:, tl:tl + HIST_ROWS, :]


def _pool_layer(x, hist, w, scale, g, b, *, pos0, nb, tl, ph):
    bt, seq, d = x.shape
    r = tl // ph
    assert bt % nb == 0 and seq % tl == 0 and tl % HIST_ROWS == 0 and r % SUBLANES == 0
    hist16 = jnp.pad(hist, ((0, 0), (HIST_ROWS - POOL_HIST, 0), (0, 0)))
    kern = functools.partial(_pool_kernel, nb=nb, tl=tl, d=d, pos0=pos0, ph=ph)
    return pl.pallas_call(
        kern,
        grid=(bt // nb, seq // tl),
        in_specs=[
            pl.BlockSpec((nb, tl, d), lambda b, l: (b, l, 0)),
            pl.BlockSpec((nb, HIST_ROWS, d), lambda b, l: (b, 0, 0)),
            _const_spec(w.shape),
            _const_spec((1, d)), _const_spec((1, d)), _const_spec((1, d)),
        ],
        out_specs=[
            pl.BlockSpec((nb, r, ph * d), lambda b, l: (b, l, 0)),
            pl.BlockSpec((nb, POOL_HIST, d), lambda b, l: (b, 0, 0)),
        ],
        out_shape=[
            jax.ShapeDtypeStruct((bt, seq // ph, ph * d), jnp.float32),
            jax.ShapeDtypeStruct((bt, POOL_HIST, d), jnp.float32),
        ],
        scratch_shapes=[
            pltpu.VMEM((nb, HIST_ROWS + tl, d), jnp.float32),
            pltpu.VMEM((d // LANES, nb * tl, LANES), jnp.float32),
        ],
        compiler_params=pltpu.CompilerParams(
            dimension_semantics=("arbitrary", "arbitrary"), vmem_limit_bytes=VMEM_LIMIT_BYTES),
        name="pool_mixer",
    )(x, hist16, w.astype(jnp.bfloat16), scale.reshape(1, d), g.reshape(1, d), b.reshape(1, d))


def _ffn_kernel(x_ref, *refs, nb, tl, d, dff, fc, ph, p_tiles, time_order_out):
    p_refs = refs[:p_tiles]
    (hist_ref, wup_ref, cw_ref, cb_ref, wdown_ref, g_ref, b_ref, wg_ref, bg_ref, wp_ref,
     out_ref, newhist_ref, hist_s, work_s, acc_s, slab_s) = refs[p_tiles:]
    l = pl.program_id(1)
    nl = pl.num_programs(1)
    nc = dff // fc
    t = nb * tl
    r = tl // ph

    @pl.when(l == 0)
    def _():
        for j in range(2 * nc):
            hist_s[j] = hist_ref[:, :, j * fc:(j + 1) * fc]

    x = _phase_rows(x_ref, ph).reshape(t, d)
    xb = x.astype(jnp.bfloat16)

    def up_project(j):
        for half in range(2):
            work = work_s.at[2 * (j % 2) + half]
            work[:, :, SUBLANES:, :] = _bdot(xb, wup_ref[half, j]).reshape(nb, ph, r, fc)

    up_project(0)
    for j in range(nc):
        if j + 1 < nc:
            up_project(j + 1)
        halves = []
        for half in range(2):
            cols = slice(half * dff + j * fc, half * dff + (j + 1) * fc)
            outs = _phase_conv(work_s.at[2 * (j % 2) + half], hist_s.at[half * nc + j], cw_ref, cb_ref,
                               cols, FFN_CONV)
            halves.append(jnp.concatenate(outs, axis=1).reshape(t, fc))
        part = _bdot(_silu(halves[0]) * halves[1], wdown_ref[j])
        if j == 0:
            acc_s[...] = part
        else:
            acc_s[...] += part

    x2 = _layer_norm(ALPHA * x + acc_s[...], g_ref[...], b_ref[...])
    gate = _sigmoid(_bdot(x2, wg_ref[...]) + bg_ref[...])
    pp = _bdot(_phase_rows_strided(p_refs, tl, ph).reshape(t, p_tiles * LANES), wp_ref[...])
    res = (x2 + gate * pp).reshape(nb, ph, r, d)
    if time_order_out:
        _store_time_order(out_ref, slab_s, res, nb, tl, ph)
    else:
        for a in range(ph):
            out_ref[:, :, a * d:(a + 1) * d] = res[:, a]

    @pl.when(l == nl - 1)
    def _():
        for j in range(2 * nc):
            for i in range(FFN_CONV - 1):
                row = (i + 1) * SUBLANES - 1
                newhist_ref[:, i:i + 1, j * fc:(j + 1) * fc] = hist_s[j, :, row:row + 1, :]


def _ffn_layer(x, p, hist, w_up, conv_w, conv_b, w_down, g, b, gate_w, gate_b, ple_proj, *, nb, tl, ph,
               time_order_out):
    bt, rows, phd = x.shape
    seq, d = rows * ph, phd // ph
    pd = p.shape[-1]
    dff = w_down.shape[0]
    fc = 2 * LANES
    front = (FFN_CONV - 1) * SUBLANES
    r = tl // ph
    assert bt % nb == 0 and seq % tl == 0 and tl % ph == 0 and r % SUBLANES == 0 and dff % fc == 0
    assert ph >= FFN_CONV - 1 and pd % LANES == 0 and p.shape[1] == seq
    nc = dff // fc
    p_tiles = pd // LANES
    hist_rep = jnp.repeat(hist, SUBLANES, axis=1)
    wup = w_up.astype(jnp.bfloat16).reshape(d, 2, nc, fc).transpose(1, 2, 0, 3)
    wdown = w_down.astype(jnp.bfloat16).reshape(nc, fc, d)
    kern = functools.partial(_ffn_kernel, nb=nb, tl=tl, d=d, dff=dff, fc=fc, ph=ph, p_tiles=p_tiles,
                             time_order_out=time_order_out)
    if time_order_out:
        out_block, out_shape, slab_rows = (nb, tl, d), (bt, seq, d), nb * tl
    else:
        out_block, out_shape, slab_rows = (nb, r, ph * d), (bt, rows, ph * d), SUBLANES
    return pl.pallas_call(
        kern,
        grid=(bt // nb, seq // tl),
        in_specs=[
            pl.BlockSpec((nb, r, ph * d), lambda b, l: (b, l, 0)),
            *[pl.BlockSpec((nb, tl, LANES), functools.partial(lambda b, l, k: (b, l, k), k=k))
              for k in range(p_tiles)],
            pl.BlockSpec((nb, front, 2 * dff), lambda b, l: (b, 0, 0)),
            _const_spec(wup.shape),
            _const_spec((FFN_CONV, 2 * dff)), _const_spec((1, 2 * dff)),
            _const_spec(wdown.shape),
            _const_spec((1, d)), _const_spec((1, d)),
            _const_spec((d, d)), _const_spec((1, d)), _const_spec((pd, d)),
        ],
        out_specs=[
            pl.BlockSpec(out_block, lambda b, l: (b, l, 0)),
            pl.BlockSpec((nb, FFN_CONV - 1, 2 * dff), lambda b, l: (b, 0, 0)),
        ],
        out_shape=[
            jax.ShapeDtypeStruct(out_shape, jnp.float32),
            jax.ShapeDtypeStruct((bt, FFN_CONV - 1, 2 * dff), jnp.float32),
        ],
        scratch_shapes=[
            pltpu.VMEM((2 * nc, nb, front, fc), jnp.float32),
            pltpu.VMEM((4, nb, ph, SUBLANES + r, fc), jnp.float32),
            pltpu.VMEM((nb * tl, d), jnp.float32),
            pltpu.VMEM((d // LANES, slab_rows, LANES), jnp.float32),
        ],
        compiler_params=pltpu.CompilerParams(
            dimension_semantics=("arbitrary", "arbitrary"), vmem_limit_bytes=VMEM_LIMIT_BYTES),
        name="conv_ffn",
    )(x, *([p] * p_tiles), hist_rep, wup, conv_w, conv_b.reshape(1, 2 * dff), wdown, g.reshape(1, d),
      b.reshape(1, d), gate_w.astype(jnp.bfloat16), gate_b.reshape(1, d), ple_proj.astype(jnp.bfloat16))


def _ssd_kernel(x_ref, hist_ref, s0_ref, wz_ref, wxbc_ref, wdt_ref, cw_ref, cb_ref, dtb_ref, alog_ref,
                dskip_ref, nw_ref, wout_ref, g_ref, b_ref, tri_ref, eq_ref, ep_ref,
                out_ref, newhist_ref, state_ref,
                hist_s, work_s, xs_s, b_s, c_s, dt_s, acs_s, cfq_s, cfp_s, dtp_s, y_s, yb_s,
                *, nb, tl, d, q):
    l = pl.program_id(1)
    nl = pl.num_programs(1)
    t = nb * tl
    d_inner = N_SSM_HEADS * HEAD_DIM
    gn = N_SSM_GROUPS * D_STATE
    conv_dim = d_inner + 2 * gn
    ph = q // SUBLANES
    cpb = tl // q
    qw = HEADS_PER_GROUP * q
    cc = 4 * LANES
    bf16 = jnp.bfloat16

    @pl.when(l == 0)
    def _():
        hist_s[...] = hist_ref[...]
        state_ref[...] = s0_ref[...]

    tiles = [(n, c, a) for n in range(nb) for c in range(cpb) for a in range(ph)]

    def tile_rows(k):
        return slice(k * SUBLANES, (k + 1) * SUBLANES)

    x = jnp.concatenate(
        [x_ref[n, c * SUBLANES:(c + 1) * SUBLANES, a * d:(a + 1) * d] for n, c, a in tiles], axis=0)
    xb = x.astype(bf16)

    def project(jc):
        xbc = _bdot(xb, wxbc_ref[:, jc * cc:(jc + 1) * cc])
        for k, (n, c, a) in enumerate(tiles):
            work_s[jc % 2, n, a, (c + 1) * SUBLANES:(c + 2) * SUBLANES, :] = xbc[tile_rows(k), :]

    project(0)
    for jc in range(conv_dim // cc):
        cols = slice(jc * cc, (jc + 1) * cc)
        if (jc + 1) * cc < conv_dim:
            project(jc + 1)
        outs = _phase_conv(work_s.at[jc % 2], hist_s.at[jc], cw_ref, cb_ref, cols, SSM_CONV)
        outs = [_silu(v) for v in outs]
        if (jc + 1) * cc <= d_inner:
            dest, off = xs_s, jc * cc
        elif (jc + 1) * cc <= d_inner + gn:
            dest, off = b_s, jc * cc - d_inner
        else:
            dest, off = c_s, jc * cc - d_inner - gn
        for k, (n, c, a) in enumerate(tiles):
            dest[tile_rows(k), off:off + cc] = outs[a][n, c * SUBLANES:(c + 1) * SUBLANES, :]

    dt = _bdot(xb, wdt_ref[...]) + dtb_ref[...]
    dt = jnp.maximum(dt, 0.0) + jnp.log1p(jnp.exp(-jnp.abs(dt)))
    dt_s[...] = dt
    a_row = -jnp.exp(alog_ref[...])
    lane = lax.broadcasted_iota(jnp.int32, (q, LANES), 1)

    def lane_split(v):
        hi, mid, lo_ = _split3(v)
        return jnp.where(lane < N_SSM_HEADS, hi, jnp.where(lane < 2 * N_SSM_HEADS, mid, lo_))

    for i in range(nb * cpb):
        rows = slice(i * q, (i + 1) * q)
        a = dt_s[rows, :] * a_row
        hi, mid, lo_ = _split3(a)
        stacked = jnp.concatenate([hi, mid, lo_, jnp.zeros_like(hi)], axis=0).astype(bf16)
        acs = jnp.dot(tri_ref[...], stacked, preferred_element_type=jnp.float32)
        acs_s[rows, :] = lane_split(acs)
        dt_s[rows, :] = lane_split(dt_s[rows, :])
    acs3 = acs_s[...].astype(bf16)
    cfq_s[...] = jnp.dot(acs3, eq_ref[...], preferred_element_type=jnp.float32)
    if q != HEAD_DIM:
        cfp_s[...] = jnp.dot(acs3, ep_ref[...], preferred_element_type=jnp.float32)
    dtp_s[...] = jnp.dot(dt_s[...].astype(bf16), ep_ref[...], preferred_element_type=jnp.float32)
    cfp = cfq_s if q == HEAD_DIM else cfp_s

    def chunk_time(idx):
        return ph * (idx & (SUBLANES - 1)) + (idx >> 3)

    row_q = lax.broadcasted_iota(jnp.int32, (q, qw), 0)
    lane_q = lax.broadcasted_iota(jnp.int32, (q, qw), 1) & (q - 1)
    eye_t = row_q == lane_q
    causal = chunk_time(lane_q) <= chunk_time(row_q)
    bd_mask = ((lax.broadcasted_iota(jnp.int32, (qw, GROUP_W), 0) >> (q.bit_length() - 1))
               == (lax.broadcasted_iota(jnp.int32, (qw, GROUP_W), 1) >> (HEAD_DIM.bit_length() - 1)))
    pad_rows = LANES - q

    def chunk_body(i, carry):
        n = i // cpb
        rows = pl.ds(pl.multiple_of(i * q, q), q)
        for g in range(N_SSM_GROUPS):
            ncols = slice(g * D_STATE, (g + 1) * D_STATE)
            pcols = slice(g * GROUP_W, (g + 1) * GROUP_W)
            bg = b_s[rows, ncols]
            cg = c_s[rows, ncols].astype(bf16)
            cf = cfq_s[rows, g * qw:(g + 1) * qw]
            cfg = cfp[rows, pcols]
            xdt = xs_s[rows, pcols] * dtp_s[rows, pcols]
            rowf = jnp.sum(jnp.where(eye_t, cf, 0.0), axis=0, keepdims=True)
            lm = jnp.where(causal, jnp.exp(cf - rowf), 0.0)
            b4 = jnp.concatenate([bg.astype(bf16)] * HEADS_PER_GROUP, axis=0)
            cb4 = lax.dot_general(cg, b4, (((1,), (1,)), ((), ())), preferred_element_type=jnp.float32)
            m = (cb4 * lm).astype(bf16)
            xbd = jnp.where(bd_mask, jnp.concatenate([xdt] * HEADS_PER_GROUP, axis=0), 0.0).astype(bf16)
            y_intra = jnp.dot(m, xbd, preferred_element_type=jnp.float32)
            st = state_ref[n, g]
            y_inter = jnp.dot(cg, st.astype(bf16), preferred_element_type=jnp.float32) * jnp.exp(cfg)
            y_s[rows, pcols] = y_intra + y_inter
            last = cfg[q - 1:q, :]
            xd = xdt * jnp.exp(last - cfg)
            if pad_rows:
                bgp = jnp.concatenate([bg, jnp.zeros((pad_rows, D_STATE), jnp.float32)], axis=0)
                xd = jnp.concatenate([xd, jnp.zeros((pad_rows, GROUP_W), jnp.float32)], axis=0)
            else:
                bgp = bg
            upd = jnp.dot(bgp.T.astype(bf16), xd.astype(bf16), preferred_element_type=jnp.float32)
            state_ref[n, g] = st * jnp.exp(last) + upd
        return carry

    lax.fori_loop(0, nb * cpb, chunk_body, 0)

    z_next = _bdot(xb, wz_ref[:, 0:GROUP_W])
    for g in range(N_SSM_GROUPS):
        pcols = slice(g * GROUP_W, (g + 1) * GROUP_W)
        z = z_next
        if g + 1 < N_SSM_GROUPS:
            z_next = _bdot(xb, wz_ref[:, (g + 1) * GROUP_W:(g + 2) * GROUP_W])
        yg = y_s[:, pcols] + dskip_ref[:, pcols] * xs_s[:, pcols]
        yg = yg * _silu(z)
        yg = yg * lax.rsqrt(jnp.mean(yg * yg, axis=-1, keepdims=True) + RMS_EPS)
        yb_s[:, pcols] = (yg * nw_ref[:, pcols]).astype(bf16)
    mix = jnp.dot(yb_s[...], wout_ref[...], preferred_element_type=jnp.float32)
    res = _layer_norm(ALPHA * x + mix, g_ref[...], b_ref[...])
    for k, (n, c, a) in enumerate(tiles):
        out_ref[n, c * SUBLANES:(c + 1) * SUBLANES, a * d:(a + 1) * d] = res[tile_rows(k), :]

    @pl.when(l == nl - 1)
    def _():
        for jc in range(conv_dim // cc):
            for i in range(SSM_CONV - 1):
                row = (i + 1) * SUBLANES - 1
                newhist_ref[:, i:i + 1, jc * cc:(jc + 1) * cc] = hist_s[jc, :, row:row + 1, :]


def _expand_matrix(rep):
    k = lax.broadcasted_iota(jnp.int32, (LANES, N_SSM_HEADS * rep), 0)
    j = lax.broadcasted_iota(jnp.int32, (LANES, N_SSM_HEADS * rep), 1)
    return ((k < 3 * N_SSM_HEADS) & (k % N_SSM_HEADS == j // rep)).astype(jnp.bfloat16)


def _ssd_layer(x, conv_hist, state, in_proj, conv_w, conv_b, dt_bias, a_log, d_skip, norm_w, out_proj,
               g, b, *, q, nb, tl):
    bt, rows, phd = x.shape
    seq, d = rows * (q // SUBLANES), phd // (q // SUBLANES)
    d_inner = N_SSM_HEADS * HEAD_DIM
    conv_dim = d_inner + 2 * N_SSM_GROUPS * D_STATE
    assert bt % nb == 0 and seq % tl == 0 and tl % q == 0 and q % SUBLANES == 0 and q <= LANES
    assert q & (q - 1) == 0, "chunk length must be a power of two"
    t = nb * tl
    qw = HEADS_PER_GROUP * q
    cc = 4 * LANES
    front = (SSM_CONV - 1) * SUBLANES
    bf16 = jnp.bfloat16

    wz = in_proj[:, :d_inner].astype(bf16)
    wxbc = in_proj[:, d_inner:d_inner + conv_dim].astype(bf16)
    wdt = in_proj[:, d_inner + conv_dim:].astype(bf16)
    pad = LANES - 3 * N_SSM_HEADS
    rep3 = lambda v: jnp.pad(jnp.concatenate([v] * 3, axis=-1), ((0, 0), (0, pad)))
    wdt3 = rep3(wdt)
    dtb3 = rep3(dt_bias.reshape(1, N_SSM_HEADS))
    alog3 = rep3(a_log.reshape(1, N_SSM_HEADS))
    dskip = jnp.repeat(d_skip, HEAD_DIM).reshape(1, d_inner)
    hist_rep = jnp.repeat(conv_hist, SUBLANES, axis=1).reshape(bt, front, conv_dim // cc, cc)
    hist_rep = hist_rep.transpose(2, 0, 1, 3)
    s0 = state.reshape(bt, N_SSM_GROUPS, HEADS_PER_GROUP, HEAD_DIM, D_STATE)
    s0 = s0.transpose(0, 1, 4, 2, 3).reshape(bt, N_SSM_GROUPS, D_STATE, GROUP_W)
    ph = q // SUBLANES
    r = tl // ph
    chunk_time = lambda idx: ph * (idx % SUBLANES) + idx // SUBLANES
    tri_col = lax.broadcasted_iota(jnp.int32, (q, 4 * q), 1)
    tri = chunk_time(tri_col % q) <= chunk_time(lax.broadcasted_iota(jnp.int32, (q, 4 * q), 0))
    tri = (tri & (tri_col < 3 * q)).astype(bf16)
    eq = _expand_matrix(q)
    ep = _expand_matrix(HEAD_DIM)

    kern = functools.partial(_ssd_kernel, nb=nb, tl=tl, d=d, q=q)
    f32 = jnp.float32
    out, newhist, newstate = pl.pallas_call(
        kern,
        grid=(bt // nb, seq // tl),
        in_specs=[
            pl.BlockSpec((nb, r, ph * d), lambda b, l: (b, l, 0)),
            pl.BlockSpec((conv_dim // cc, nb, front, cc), lambda b, l: (0, b, 0, 0)),
            pl.BlockSpec((nb, N_SSM_GROUPS, D_STATE, GROUP_W), lambda b, l: (b, 0, 0, 0)),
            _const_spec(wz.shape), _const_spec(wxbc.shape), _const_spec(wdt3.shape),
            _const_spec((SSM_CONV, conv_dim)), _const_spec((1, conv_dim)),
            _const_spec((1, LANES)), _const_spec((1, LANES)),
            _const_spec((1, d_inner)), _const_spec((1, d_inner)),
            _const_spec((d_inner, d)), _const_spec((1, d)), _const_spec((1, d)),
            _const_spec(tri.shape), _const_spec(eq.shape), _const_spec(ep.shape),
        ],
        out_specs=[
            pl.BlockSpec((nb, r, ph * d), lambda b, l: (b, l, 0)),
            pl.BlockSpec((nb, SSM_CONV - 1, conv_dim), lambda b, l: (b, 0, 0)),
            pl.BlockSpec((nb, N_SSM_GROUPS, D_STATE, GROUP_W), lambda b, l: (b, 0, 0, 0)),
        ],
        out_shape=[
            jax.ShapeDtypeStruct((bt, seq // ph, ph * d), f32),
            jax.ShapeDtypeStruct((bt, SSM_CONV - 1, conv_dim), f32),
            jax.ShapeDtypeStruct((bt, N_SSM_GROUPS, D_STATE, GROUP_W), f32),
        ],
        scratch_shapes=[
            pltpu.VMEM((conv_dim // cc, nb, front, cc), f32),
            pltpu.VMEM((2, nb, ph, SUBLANES + r, cc), f32),
            pltpu.VMEM((t, d_inner), f32),
            pltpu.VMEM((t, N_SSM_GROUPS * D_STATE), f32),
            pltpu.VMEM((t, N_SSM_GROUPS * D_STATE), f32),
            pltpu.VMEM((t, LANES), f32),
            pltpu.VMEM((t, LANES), f32),
            pltpu.VMEM((t, N_SSM_GROUPS * qw), f32),
            pltpu.VMEM((t, d_inner) if q != HEAD_DIM else (SUBLANES, LANES), f32),
            pltpu.VMEM((t, d_inner), f32),
            pltpu.VMEM((t, d_inner), f32),
            pltpu.VMEM((t, d_inner), bf16),
        ],
        compiler_params=pltpu.CompilerParams(
            dimension_semantics=("arbitrary", "arbitrary"), vmem_limit_bytes=VMEM_LIMIT_BYTES),
        name="ssd_mixer",
    )(x, hist_rep, s0, wz, wxbc, wdt3, conv_w, conv_b.reshape(1, conv_dim),
      dtb3, alog3, dskip, norm_w.reshape(1, d_inner), out_proj.astype(bf16), g.reshape(1, d),
      b.reshape(1, d), tri, eq, ep)
    newstate = newstate.reshape(bt, N_SSM_GROUPS, D_STATE, HEADS_PER_GROUP, HEAD_DIM)
    newstate = newstate.transpose(0, 1, 3, 4, 2).reshape(bt, N_SSM_HEADS, HEAD_DIM, D_STATE)
    return out, newhist, newstate


def _trunk(x, p, pool_hist, ssm_conv_hist, ssm_state, ffn_hist, w, *, pos0, q, pool_blk, ssd_blk, ffn_blk):
    ph = q // SUBLANES
    assert ffn_blk[2] == ph
    x, new_pool = _pool_layer(x, pool_hist[0], w['pool_w'][0], w['pool_scale'][0], w['ln_mix_g'][0],
                              w['ln_mix_b'][0], pos0=pos0, nb=pool_blk[0], tl=pool_blk[1], ph=ph)
    new_ffn = []
    x, fh = _ffn_layer(x, p[0], ffn_hist[0], w['ffn_up'][0], w['ffn_conv_w'][0], w['ffn_conv_b'][0],
                       w['ffn_down'][0], w['ln_ffn_g'][0], w['ln_ffn_b'][0], w['ple_gate_w'][0],
                       w['ple_gate_b'][0], w['ple_proj'][0], nb=ffn_blk[0], tl=ffn_blk[1], ph=ph,
                       time_order_out=False)
    new_ffn.append(fh)
    x, new_conv, new_state = _ssd_layer(
        x, ssm_conv_hist[0], ssm_state[0], w['ssm_in_proj'][0], w['ssm_conv_w'][0], w['ssm_conv_b'][0],
        w['ssm_dt_bias'][0], w['ssm_A_log'][0], w['ssm_D'][0], w['ssm_norm_w'][0], w['ssm_out_proj'][0],
        w['ln_mix_g'][1], w['ln_mix_b'][1], q=q, nb=ssd_blk[0], tl=ssd_blk[1])
    x, fh = _ffn_layer(x, p[1], ffn_hist[1], w['ffn_up'][1], w['ffn_conv_w'][1], w['ffn_conv_b'][1],
                       w['ffn_down'][1], w['ln_ffn_g'][1], w['ln_ffn_b'][1], w['ple_gate_w'][1],
                       w['ple_gate_b'][1], w['ple_proj'][1], nb=ffn_blk[0], tl=ffn_blk[1], ph=ph,
                       time_order_out=True)
    new_ffn.append(fh)
    return x, new_pool[None], new_conv[None], new_state[None], jnp.stack(new_ffn)


def _phases(tl):
    ph = SUBLANES
    while tl % (ph * SUBLANES):
        ph //= 2
    return ph


def _block_len(seq, target):
    tl = min(seq, target)
    while seq % tl:
        tl //= 2
    return tl


def kernel(x_prompt, x_sample, p_prompt, p_sample, cache_pool, cache_ssm_conv, state_ssm, cache_ffn_conv,
           pool_w, pool_scale, ssm_in_proj, ssm_conv_w, ssm_conv_b, ssm_dt_bias, ssm_A_log, ssm_D, ssm_norm_w,
           ssm_out_proj, ln_mix_g, ln_mix_b, ffn_up, ffn_conv_w, ffn_conv_b, ffn_down, ln_ffn_g, ln_ffn_b,
           ple_proj, ple_gate_w, ple_gate_b):
    assert pool_w.shape[0] == 1 and ssm_in_proj.shape[0] == 1 and ffn_up.shape[0] == DEPTH
    w = dict(pool_w=pool_w, pool_scale=pool_scale, ssm_in_proj=ssm_in_proj, ssm_conv_w=ssm_conv_w,
             ssm_conv_b=ssm_conv_b, ssm_dt_bias=ssm_dt_bias, ssm_A_log=ssm_A_log, ssm_D=ssm_D,
             ssm_norm_w=ssm_norm_w, ssm_out_proj=ssm_out_proj, ln_mix_g=ln_mix_g, ln_mix_b=ln_mix_b,
             ffn_up=ffn_up, ffn_conv_w=ffn_conv_w, ffn_conv_b=ffn_conv_b, ffn_down=ffn_down,
             ln_ffn_g=ln_ffn_g, ln_ffn_b=ln_ffn_b, ple_proj=ple_proj, ple_gate_w=ple_gate_w,
             ple_gate_b=ple_gate_b)
    bp, seq, d = x_prompt.shape
    bs, dseq, _ = x_sample.shape
    f32 = x_prompt.dtype
    conv_dim = ssm_conv_w.shape[-1]
    z_pool = jnp.zeros((1, bp, POOL_HIST, d), f32)
    z_sconv = jnp.zeros((1, bp, SSM_CONV - 1, conv_dim), f32)
    z_state = jnp.zeros((1, bp, N_SSM_HEADS, HEAD_DIM, D_STATE), f32)
    z_ffn = jnp.zeros((DEPTH, bp, FFN_CONV - 1, ffn_conv_w.shape[-1]), f32)
    tl_p = _block_len(seq, 512)
    tl_s = _block_len(seq, 256)
    y_p, pool_p, sconv_p, state_p, ffn_p = _trunk(
        x_prompt, p_prompt, z_pool, z_sconv, z_state, z_ffn, w, pos0=0, q=min(SSD_BLOCK, seq),
        pool_blk=(1, tl_p), ssd_blk=(1, tl_s), ffn_blk=(1, tl_p, _phases(tl_p)))
    y_s, pool_s, sconv_s, state_s, ffn_s = _trunk(
        x_sample, p_sample, cache_pool, cache_ssm_conv, state_ssm, cache_ffn_conv, w, pos0=PAST_LEN,
        q=dseq, pool_blk=(bs, dseq), ssd_blk=(2, dseq), ffn_blk=(bs, dseq, _phases(dseq)))
    return (y_p, y_s, pool_p, pool_s, sconv_p, sconv_s, state_p, state_s, ffn_p, ffn_s)
```

```python
import functools

import jax
import jax.numpy as jnp
from jax import lax
from jax.experimental import pallas as pl
from jax.experimental.pallas import tpu as pltpu

PAST_LEN = 1024
DEPTH = 2
ALPHA = (2 * DEPTH) ** 0.25
LN_EPS = 1e-5
RMS_EPS = 1e-5
POOL_WINDOWS = (2, 4, 8, 16)
POOL_HIST = max(POOL_WINDOWS) - 1
HEAD_DIM = 64
N_SSM_GROUPS = 8
HEADS_PER_GROUP = 4
N_SSM_HEADS = N_SSM_GROUPS * HEADS_PER_GROUP
D_STATE = 128
SSM_CONV = 4
FFN_CONV = 3
SSD_BLOCK = 64
LOG2_E = 1.4426950408889634

LANES = 128
SUBLANES = 8
VMEM_LIMIT_BYTES = 56 * 1024 * 1024

HIST_ROWS = 3 * SUBLANES
GROUP_W = HEADS_PER_GROUP * HEAD_DIM
DOWN_CHUNKS = 2
EPILOGUE_GROUPS = 2


def _bdot(a, b):
    return jnp.dot(a.astype(jnp.bfloat16), b.astype(jnp.bfloat16), preferred_element_type=jnp.float32)


def _sigmoid(v):
    return 1.0 / (1.0 + jnp.exp(-v))


def _silu(v):
    return v * _sigmoid(v)


def _layer_norm(v, g, b):
    mu = jnp.mean(v, axis=-1, keepdims=True)
    vc = v - mu
    var = jnp.mean(vc * vc, axis=-1, keepdims=True)
    return vc * lax.rsqrt(var + LN_EPS) * g + b


def _const_spec(shape):
    nd = len(shape)
    return pl.BlockSpec(shape, lambda b, l: (0,) * nd, pipeline_mode=pl.Buffered(1))


def _split3(v):
    hi = v.astype(jnp.bfloat16).astype(jnp.float32)
    r1 = v - hi
    mid = r1.astype(jnp.bfloat16).astype(jnp.float32)
    lo = (r1 - mid).astype(jnp.bfloat16).astype(jnp.float32)
    return hi, mid, lo


def _phase_rows(ref, ph):
    c = ref.shape[-1] // ph
    return jnp.concatenate([ref[:, :, a * c:(a + 1) * c] for a in range(ph)], axis=1)


def _store_phases(out_ref, slab_ref, v, nb, tl, ph):
    d = v.shape[-1]
    r = tl // ph
    for k in range(d // LANES):
        slab_ref[k] = v[:, k * LANES:(k + 1) * LANES]
    for n in range(nb):
        for a in range(ph):
            for k in range(d // LANES):
                out_ref[n, :, a * d + k * LANES:a * d + (k + 1) * LANES] = (
                    slab_ref[k, pl.ds(n * tl + a, r, stride=ph), :])


def _stage_time_order(slab_ref, v, a0, tl, ph):
    nb, phases, r, d = v.shape
    for n in range(nb):
        for a in range(phases):
            for k in range(d // LANES):
                slab_ref[k, pl.ds(n * tl + a0 + a, r, stride=ph), :] = v[n, a, :, k * LANES:(k + 1) * LANES]


def _store_slabs(out_ref, slab_ref, nb, tl):
    for k in range(slab_ref.shape[0]):
        out_ref[:, :, k * LANES:(k + 1) * LANES] = slab_ref[k].reshape(nb, tl, LANES)


def _phase_rows_strided(tile_refs, tl, ph):
    r = tl // ph
    return jnp.concatenate(
        [jnp.concatenate([ref[:, pl.ds(a, r, stride=ph), :] for ref in tile_refs], axis=-1)
         for a in range(ph)], axis=1)


def _phase_conv(work, hist, cw_ref, cb_ref, cols, taps):
    ph = work.shape[1]
    r = work.shape[2] - SUBLANES
    for i in range(taps - 1):
        a = ph - (taps - 1) + i
        rows = slice(i * SUBLANES, (i + 1) * SUBLANES)
        work[:, a, 0:SUBLANES, :] = hist[:, rows, :]
        hist[:, rows, :] = work[:, a, r:r + SUBLANES, :]
    outs = []
    for a in range(ph):
        c = cb_ref[:, cols]
        for k in range(taps):
            dist = taps - 1 - k
            if a >= dist:
                src = work[:, a - dist, SUBLANES:SUBLANES + r, :]
            else:
                src = work[:, a - dist + ph, SUBLANES - 1:SUBLANES - 1 + r, :]
            c = c + cw_ref[k:k + 1, cols] * src
        outs.append(c)
    return outs


def _pool_kernel(x_ref, hist_ref, w_ref, scale_ref, g_ref, b_ref, out_ref, newhist_ref, xh_ref, slab_ref,
                 lvl_ref, *, nb, tl, d, pos0, ph):
    l = pl.program_id(1)
    nl = pl.num_programs(1)
    gw = d // len(POOL_WINDOWS)
    lo, hi = SUBLANES, HIST_ROWS + tl

    @pl.when(l == 0)
    def _():
        xh_ref[:, 0:HIST_ROWS, :] = hist_ref[...]
        lvl_ref[:, :, 0:lo, :] = jnp.zeros((2, nb, lo, gw), jnp.float32)

    x = x_ref[...]
    xh_ref[:, HIST_ROWS:HIST_ROWS + tl, :] = x

    pos = pos0 + l * tl + lax.broadcasted_iota(jnp.int32, (1, tl, gw), 1)
    ys = []
    for gi, wsz in enumerate(POOL_WINDOWS):
        cols = slice(gi * gw, (gi + 1) * gw)
        s = xh_ref[:, lo:hi, cols] + xh_ref[:, lo - 1:hi - 1, cols]
        k, buf = 2, 0
        while k < wsz:
            lvl_ref[buf, :, lo:hi, :] = s
            s = lvl_ref[buf, :, lo:hi, :] + lvl_ref[buf, :, lo - k:hi - k, :]
            k, buf = 2 * k, 1 - buf
        s = s[:, HIST_ROWS - lo:, :]
        cnt = jnp.minimum(wsz, pos + 1).astype(jnp.float32)
        pooled = (s / cnt - x[:, :, cols]).reshape(nb * tl, gw)
        ys.append(_bdot(pooled, w_ref[gi]))
    y = jnp.concatenate(ys, axis=-1) * scale_ref[...]
    xf = x.reshape(nb * tl, d)
    out = _layer_norm(ALPHA * xf + y, g_ref[...], b_ref[...])
    _store_phases(out_ref, slab_ref, out, nb, tl, ph)

    @pl.when(l == nl - 1)
    def _():
        newhist_ref[...] = xh_ref[:, HIST_ROWS + tl - POOL_HIST:HIST_ROWS + tl, :]

    xh_ref[:, 0:HIST_ROWS, :] = xh_ref[:, tl:tl + HIST_ROWS, :]


def _pool_layer(x, hist, w, scale, g, b, *, pos0, nb, tl, ph):
    bt, seq, d = x.shape
    r = tl // ph
    assert bt % nb == 0 and seq % tl == 0 and tl >= HIST_ROWS and r % SUBLANES == 0
    hist16 = jnp.pad(hist, ((0, 0), (HIST_ROWS - POOL_HIST, 0), (0, 0)))
    kern = functools.partial(_pool_kernel, nb=nb, tl=tl, d=d, pos0=pos0, ph=ph)
    return pl.pallas_call(
        kern,
        grid=(bt // nb, seq // tl),
        in_specs=[
            pl.BlockSpec((nb, tl, d), lambda b, l: (b, l, 0)),
            pl.BlockSpec((nb, HIST_ROWS, d), lambda b, l: (b, 0, 0)),
            _const_spec(w.shape),
            _const_spec((1, d)), _const_spec((1, d)), _const_spec((1, d)),
        ],
        out_specs=[
            pl.BlockSpec((nb, r, ph * d), lambda b, l: (b, l, 0)),
            pl.BlockSpec((nb, POOL_HIST, d), lambda b, l: (b, 0, 0)),
        ],
        out_shape=[
            jax.ShapeDtypeStruct((bt, seq // ph, ph * d), jnp.float32),
            jax.ShapeDtypeStruct((bt, POOL_HIST, d), jnp.float32),
        ],
        scratch_shapes=[
            pltpu.VMEM((nb, HIST_ROWS + tl, d), jnp.float32),
            pltpu.VMEM((d // LANES, nb * tl, LANES), jnp.float32),
            pltpu.VMEM((2, nb, HIST_ROWS + tl, d // len(POOL_WINDOWS)), jnp.float32),
        ],
        compiler_params=pltpu.CompilerParams(
            dimension_semantics=("arbitrary", "arbitrary"), vmem_limit_bytes=VMEM_LIMIT_BYTES),
        name="pool_mixer",
    )(x, hist16, w.astype(jnp.bfloat16), scale.reshape(1, d), g.reshape(1, d), b.reshape(1, d))


def _ffn_kernel(x_ref, *refs, nb, tl, d, dff, fc, ph, p_tiles, time_order_out):
    p_refs = refs[:p_tiles]
    (hist_ref, wup_ref, cw_ref, cb_ref, wdown_ref, g_ref, b_ref, wg_ref, bg_ref, wp_ref,
     out_ref, newhist_ref, hist_s, work_s, acc_s, slab_s) = refs[p_tiles:]
    l = pl.program_id(1)
    nl = pl.num_programs(1)
    nc = dff // fc
    t = nb * tl
    r = tl // ph

    @pl.when(l == 0)
    def _():
        for j in range(2 * nc):
            hist_s[j] = hist_ref[:, :, j * fc:(j + 1) * fc]

    x = _phase_rows(x_ref, ph).reshape(t, d)
    xb = x.astype(jnp.bfloat16)

    def up_project(j):
        for half in range(2):
            work = work_s.at[2 * (j % 2) + half]
            cols = slice(half * dff + j * fc, half * dff + (j + 1) * fc)
            work[:, :, SUBLANES:, :] = _bdot(xb, wup_ref[:, cols]).reshape(nb, ph, r, fc)

    up_project(0)
    pending = []
    for j in range(nc):
        if j + 1 < nc:
            up_project(j + 1)
        halves = []
        for half in range(2):
            cols = slice(half * dff + j * fc, half * dff + (j + 1) * fc)
            outs = _phase_conv(work_s.at[2 * (j % 2) + half], hist_s.at[half * nc + j], cw_ref, cb_ref,
                               cols, FFN_CONV)
            halves.append(jnp.concatenate(outs, axis=1).reshape(t, fc))
        pending.append((_silu(halves[0]) * halves[1]).astype(jnp.bfloat16))
        if len(pending) == DOWN_CHUNKS or j == nc - 1:
            j0 = j + 1 - len(pending)
            part = jnp.dot(jnp.concatenate(pending, axis=1), wdown_ref[j0 * fc:(j + 1) * fc, :],
                           preferred_element_type=jnp.float32)
            if j0 == 0:
                acc_s[...] = part
            else:
                acc_s[...] += part
            pending = []

    x4 = x.reshape(nb, ph, r, d)
    acc4 = acc_s[...].reshape(nb, ph, r, d)
    p4 = _phase_rows_strided(p_refs, tl, ph).reshape(nb, ph, r, p_tiles * LANES)
    gsz = max(ph // EPILOGUE_GROUPS, 1)
    for a0 in range(0, ph, gsz):
        rows = nb * gsz * r
        pre = ALPHA * x4[:, a0:a0 + gsz] + acc4[:, a0:a0 + gsz]
        x2 = _layer_norm(pre.reshape(rows, d), g_ref[...], b_ref[...])
        gate = _sigmoid(_bdot(x2, wg_ref[...]) + bg_ref[...])
        pp = _bdot(p4[:, a0:a0 + gsz].reshape(rows, p_tiles * LANES), wp_ref[...])
        res = (x2 + gate * pp).reshape(nb, gsz, r, d)
        if time_order_out:
            _stage_time_order(slab_s, res, a0, tl, ph)
        else:
            for a in range(gsz):
                out_ref[:, :, (a0 + a) * d:(a0 + a + 1) * d] = res[:, a]
    if time_order_out:
        _store_slabs(out_ref, slab_s, nb, tl)

    @pl.when(l == nl - 1)
    def _():
        for j in range(2 * nc):
            for i in range(FFN_CONV - 1):
                row = (i + 1) * SUBLANES - 1
                newhist_ref[:, i:i + 1, j * fc:(j + 1) * fc] = hist_s[j, :, row:row + 1, :]


def _ffn_layer(x, p, hist, w_up, conv_w, conv_b, w_down, g, b, gate_w, gate_b, ple_proj, *, nb, tl, ph,
               time_order_out):
    bt, rows, phd = x.shape
    seq, d = rows * ph, phd // ph
    pd = p.shape[-1]
    dff = w_down.shape[0]
    fc = 2 * LANES
    front = (FFN_CONV - 1) * SUBLANES
    r = tl // ph
    assert bt % nb == 0 and seq % tl == 0 and tl % ph == 0 and r % SUBLANES == 0 and dff % fc == 0
    assert ph >= FFN_CONV - 1 and pd % LANES == 0 and p.shape[1] == seq
    nc = dff // fc
    p_tiles = pd // LANES
    hist_rep = jnp.repeat(hist, SUBLANES, axis=1)
    wup = w_up.astype(jnp.bfloat16)
    wdown = w_down.astype(jnp.bfloat16)
    kern = functools.partial(_ffn_kernel, nb=nb, tl=tl, d=d, dff=dff, fc=fc, ph=ph, p_tiles=p_tiles,
                             time_order_out=time_order_out)
    if time_order_out:
        out_block, out_shape, slab_rows = (nb, tl, d), (bt, seq, d), nb * tl
    else:
        out_block, out_shape, slab_rows = (nb, r, ph * d), (bt, rows, ph * d), SUBLANES
    return pl.pallas_call(
        kern,
        grid=(bt // nb, seq // tl),
        in_specs=[
            pl.BlockSpec((nb, r, ph * d), lambda b, l: (b, l, 0)),
            *[pl.BlockSpec((nb, tl, LANES), functools.partial(lambda b, l, k: (b, l, k), k=k))
              for k in range(p_tiles)],
            pl.BlockSpec((nb, front, 2 * dff), lambda b, l: (b, 0, 0)),
            _const_spec(wup.shape),
            _const_spec((FFN_CONV, 2 * dff)), _const_spec((1, 2 * dff)),
            _const_spec(wdown.shape),
            _const_spec((1, d)), _const_spec((1, d)),
            _const_spec((d, d)), _const_spec((1, d)), _const_spec((pd, d)),
        ],
        out_specs=[
            pl.BlockSpec(out_block, lambda b, l: (b, l, 0)),
            pl.BlockSpec((nb, FFN_CONV - 1, 2 * dff), lambda b, l: (b, 0, 0)),
        ],
        out_shape=[
            jax.ShapeDtypeStruct(out_shape, jnp.float32),
            jax.ShapeDtypeStruct((bt, FFN_CONV - 1, 2 * dff), jnp.float32),
        ],
        scratch_shapes=[
            pltpu.VMEM((2 * nc, nb, front, fc), jnp.float32),
            pltpu.VMEM((4, nb, ph, SUBLANES + r, fc), jnp.float32),
            pltpu.VMEM((nb * tl, d), jnp.float32),
            pltpu.VMEM((d // LANES, slab_rows, LANES), jnp.float32),
        ],
        compiler_params=pltpu.CompilerParams(
            dimension_semantics=("arbitrary", "arbitrary"), vmem_limit_bytes=VMEM_LIMIT_BYTES),
        name="conv_ffn",
    )(x, *([p] * p_tiles), hist_rep, wup, conv_w, conv_b.reshape(1, 2 * dff), wdown, g.reshape(1, d),
      b.reshape(1, d), gate_w.astype(jnp.bfloat16), gate_b.reshape(1, d), ple_proj.astype(jnp.bfloat16))


def _ssd_kernel(x_ref, hist_ref, s0_ref, wz_ref, wxbc_ref, wdt_ref, cw_ref, cb_ref, dtb_ref, alog_ref,
                dskip_ref, nw_ref, wout_ref, g_ref, b_ref, tri_ref, eq_ref, ep_ref,
                out_ref, newhist_ref, state_ref,
                hist_s, work_s, xs_s, b_s, c_s, dt_s, acs_s, cfq_s, cfp_s, dtp_s, z_s, yb_s,
                *, nb, tl, d, q):
    l = pl.program_id(1)
    nl = pl.num_programs(1)
    t = nb * tl
    d_inner = N_SSM_HEADS * HEAD_DIM
    gn = N_SSM_GROUPS * D_STATE
    conv_dim = d_inner + 2 * gn
    ph = q // SUBLANES
    cpb = tl // q
    qw = HEADS_PER_GROUP * q
    cc = 4 * LANES
    bf16 = jnp.bfloat16

    @pl.when(l == 0)
    def _():
        hist_s[...] = hist_ref[...]
        state_ref[...] = s0_ref[...]

    tiles = [(n, c, a) for n in range(nb) for c in range(cpb) for a in range(ph)]

    def tile_rows(k):
        return slice(k * SUBLANES, (k + 1) * SUBLANES)

    x = jnp.concatenate(
        [x_ref[n, c * SUBLANES:(c + 1) * SUBLANES, a * d:(a + 1) * d] for n, c, a in tiles], axis=0)
    xb = x.astype(bf16)

    dt = _bdot(xb, wdt_ref[...]) + dtb_ref[...]
    dt = jnp.maximum(dt, 0.0) + jnp.log1p(jnp.exp(-jnp.abs(dt)))
    dt_s[...] = dt
    a_row = -jnp.exp(alog_ref[...]) * LOG2_E
    lane = lax.broadcasted_iota(jnp.int32, (q, LANES), 1)

    def lane_split(v):
        hi, mid, lo_ = _split3(v)
        return jnp.where(lane < N_SSM_HEADS, hi, jnp.where(lane < 2 * N_SSM_HEADS, mid, lo_))

    for i in range(nb * cpb):
        rows = slice(i * q, (i + 1) * q)
        a = dt_s[rows, :] * a_row
        hi, mid, lo_ = _split3(a)
        stacked = jnp.concatenate([hi, mid, lo_, jnp.zeros_like(hi)], axis=0).astype(bf16)
        acs = jnp.dot(tri_ref[...], stacked, preferred_element_type=jnp.float32)
        acs_s[rows, :] = lane_split(acs)
        dt_s[rows, :] = lane_split(dt_s[rows, :])
    acs3 = acs_s[...].astype(bf16)
    cfq_s[...] = jnp.dot(acs3, eq_ref[...], preferred_element_type=jnp.float32)
    if q != HEAD_DIM:
        cfp_s[...] = jnp.dot(acs3, ep_ref[...], preferred_element_type=jnp.float32)
    dtp_s[...] = jnp.dot(dt_s[...].astype(bf16), ep_ref[...], preferred_element_type=jnp.float32)
    cfp = cfq_s if q == HEAD_DIM else cfp_s

    def project(jc):
        xbc = _bdot(xb, wxbc_ref[:, jc * cc:(jc + 1) * cc])
        for k, (n, c, a) in enumerate(tiles):
            work_s[jc % 2, n, a, (c + 1) * SUBLANES:(c + 2) * SUBLANES, :] = xbc[tile_rows(k), :]

    project(0)
    for jc in range(conv_dim // cc):
        cols = slice(jc * cc, (jc + 1) * cc)
        if (jc + 1) * cc < conv_dim:
            project(jc + 1)
        if jc < N_SSM_GROUPS:
            z_s[:, jc * GROUP_W:(jc + 1) * GROUP_W] = _bdot(xb, wz_ref[:, jc * GROUP_W:(jc + 1) * GROUP_W])
        outs = _phase_conv(work_s.at[jc % 2], hist_s.at[jc], cw_ref, cb_ref, cols, SSM_CONV)
        outs = [_silu(v) for v in outs]
        if (jc + 1) * cc <= d_inner:
            dest, off = xs_s, jc * cc
        elif (jc + 1) * cc <= d_inner + gn:
            dest, off = b_s, jc * cc - d_inner
        else:
            dest, off = c_s, jc * cc - d_inner - gn
        for k, (n, c, a) in enumerate(tiles):
            dest[tile_rows(k), off:off + cc] = outs[a][n, c * SUBLANES:(c + 1) * SUBLANES, :]

    def chunk_time(idx):
        return ph * (idx & (SUBLANES - 1)) + (idx >> 3)

    row_q = lax.broadcasted_iota(jnp.int32, (q, qw), 0)
    lane_q = lax.broadcasted_iota(jnp.int32, (q, qw), 1) & (q - 1)
    eye_t = row_q == lane_q
    causal = chunk_time(lane_q) <= chunk_time(row_q)
    bd_mask = ((lax.broadcasted_iota(jnp.int32, (qw, GROUP_W), 0) >> (q.bit_length() - 1))
               == (lax.broadcasted_iota(jnp.int32, (qw, GROUP_W), 1) >> (HEAD_DIM.bit_length() - 1)))
    bd_ones = jnp.where(bd_mask, 1.0, 0.0).astype(bf16)
    pad_rows = LANES - q

    for i in range(nb * cpb):
        n = i // cpb
        rows = slice(i * q, (i + 1) * q)
        for g in range(N_SSM_GROUPS):
            ncols = slice(g * D_STATE, (g + 1) * D_STATE)
            pcols = slice(g * GROUP_W, (g + 1) * GROUP_W)
            bg = b_s[rows, ncols]
            cg = c_s[rows, ncols].astype(bf16)
            cf = cfq_s[rows, g * qw:(g + 1) * qw]
            cfg = cfp[rows, pcols]
            xs = xs_s[rows, pcols]
            xdt = xs * dtp_s[rows, pcols]
            rowf = jnp.sum(jnp.where(eye_t, cf, 0.0), axis=0, keepdims=True)
            lm = jnp.where(causal, jnp.exp2(cf - rowf), 0.0)
            b4 = jnp.concatenate([bg.astype(bf16)] * HEADS_PER_GROUP, axis=0)
            cb4 = lax.dot_general(cg, b4, (((1,), (1,)), ((), ())), preferred_element_type=jnp.float32)
            m = (cb4 * lm).astype(bf16)
            xbd = jnp.concatenate([xdt.astype(bf16)] * HEADS_PER_GROUP, axis=0) * bd_ones
            y_intra = jnp.dot(m, xbd, preferred_element_type=jnp.float32)
            st = state_ref[n, g]
            y_inter = jnp.dot(cg, st.astype(bf16), preferred_element_type=jnp.float32) * jnp.exp2(cfg)
            yg = y_intra + y_inter + dskip_ref[:, pcols] * xs
            yg = yg * _silu(z_s[rows, pcols])
            yg = yg * lax.rsqrt(jnp.mean(yg * yg, axis=-1, keepdims=True) + RMS_EPS)
            yb_s[rows, pcols] = (yg * nw_ref[:, pcols]).astype(bf16)
            last = cfg[q - 1:q, :]
            xd = xdt * jnp.exp2(last - cfg)
            if pad_rows:
                bgp = jnp.concatenate([bg, jnp.zeros((pad_rows, D_STATE), jnp.float32)], axis=0)
                xd = jnp.concatenate([xd, jnp.zeros((pad_rows, GROUP_W), jnp.float32)], axis=0)
            else:
                bgp = bg
            upd = jnp.dot(bgp.T.astype(bf16), xd.astype(bf16), preferred_element_type=jnp.float32)
            state_ref[n, g] = st * jnp.exp2(last) + upd

    mix = jnp.dot(yb_s[...], wout_ref[...], preferred_element_type=jnp.float32)
    res = _layer_norm(ALPHA * x + mix, g_ref[...], b_ref[...])
    for k, (n, c, a) in enumerate(tiles):
        out_ref[n, c * SUBLANES:(c + 1) * SUBLANES, a * d:(a + 1) * d] = res[tile_rows(k), :]

    @pl.when(l == nl - 1)
    def _():
        for jc in range(conv_dim // cc):
            for i in range(SSM_CONV - 1):
                row = (i + 1) * SUBLANES - 1
                newhist_ref[:, i:i + 1, jc * cc:(jc + 1) * cc] = hist_s[jc, :, row:row + 1, :]


def _expand_matrix(rep):
    k = lax.broadcasted_iota(jnp.int32, (LANES, N_SSM_HEADS * rep), 0)
    j = lax.broadcasted_iota(jnp.int32, (LANES, N_SSM_HEADS * rep), 1)
    return ((k < 3 * N_SSM_HEADS) & (k % N_SSM_HEADS == j // rep)).astype(jnp.bfloat16)


def _ssd_layer(x, conv_hist, state, in_proj, conv_w, conv_b, dt_bias, a_log, d_skip, norm_w, out_proj,
               g, b, *, q, nb, tl):
    bt, rows, phd = x.shape
    seq, d = rows * (q // SUBLANES), phd // (q // SUBLANES)
    d_inner = N_SSM_HEADS * HEAD_DIM
    conv_dim = d_inner + 2 * N_SSM_GROUPS * D_STATE
    assert bt % nb == 0 and seq % tl == 0 and tl % q == 0 and q % SUBLANES == 0 and q <= LANES
    assert q & (q - 1) == 0, "chunk length must be a power of two"
    t = nb * tl
    qw = HEADS_PER_GROUP * q
    cc = 4 * LANES
    front = (SSM_CONV - 1) * SUBLANES
    bf16 = jnp.bfloat16

    wz = in_proj[:, :d_inner].astype(bf16)
    wxbc = in_proj[:, d_inner:d_inner + conv_dim].astype(bf16)
    wdt = in_proj[:, d_inner + conv_dim:].astype(bf16)
    pad = LANES - 3 * N_SSM_HEADS
    rep3 = lambda v: jnp.pad(jnp.concatenate([v] * 3, axis=-1), ((0, 0), (0, pad)))
    wdt3 = rep3(wdt)
    dtb3 = rep3(dt_bias.reshape(1, N_SSM_HEADS))
    alog3 = rep3(a_log.reshape(1, N_SSM_HEADS))
    dskip = jnp.repeat(d_skip, HEAD_DIM).reshape(1, d_inner)
    hist_rep = jnp.repeat(conv_hist, SUBLANES, axis=1).reshape(bt, front, conv_dim // cc, cc)
    hist_rep = hist_rep.transpose(2, 0, 1, 3)
    s0 = state.reshape(bt, N_SSM_GROUPS, HEADS_PER_GROUP, HEAD_DIM, D_STATE)
    s0 = s0.transpose(0, 1, 4, 2, 3).reshape(bt, N_SSM_GROUPS, D_STATE, GROUP_W)
    ph = q // SUBLANES
    r = tl // ph
    chunk_time = lambda idx: ph * (idx % SUBLANES) + idx // SUBLANES
    tri_col = lax.broadcasted_iota(jnp.int32, (q, 4 * q), 1)
    tri = chunk_time(tri_col % q) <= chunk_time(lax.broadcasted_iota(jnp.int32, (q, 4 * q), 0))
    tri = (tri & (tri_col < 3 * q)).astype(bf16)
    eq = _expand_matrix(q)
    ep = _expand_matrix(HEAD_DIM)

    kern = functools.partial(_ssd_kernel, nb=nb, tl=tl, d=d, q=q)
    f32 = jnp.float32
    out, newhist, newstate = pl.pallas_call(
        kern,
        grid=(bt // nb, seq // tl),
        in_specs=[
            pl.BlockSpec((nb, r, ph * d), lambda b, l: (b, l, 0)),
            pl.BlockSpec((conv_dim // cc, nb, front, cc), lambda b, l: (0, b, 0, 0)),
            pl.BlockSpec((nb, N_SSM_GROUPS, D_STATE, GROUP_W), lambda b, l: (b, 0, 0, 0)),
            _const_spec(wz.shape), _const_spec(wxbc.shape), _const_spec(wdt3.shape),
            _const_spec((SSM_CONV, conv_dim)), _const_spec((1, conv_dim)),
            _const_spec((1, LANES)), _const_spec((1, LANES)),
            _const_spec((1, d_inner)), _const_spec((1, d_inner)),
            _const_spec((d_inner, d)), _const_spec((1, d)), _const_spec((1, d)),
            _const_spec(tri.shape), _const_spec(eq.shape), _const_spec(ep.shape),
        ],
        out_specs=[
            pl.BlockSpec((nb, r, ph * d), lambda b, l: (b, l, 0)),
            pl.BlockSpec((nb, SSM_CONV - 1, conv_dim), lambda b, l: (b, 0, 0)),
            pl.BlockSpec((nb, N_SSM_GROUPS, D_STATE, GROUP_W), lambda b, l: (b, 0, 0, 0)),
        ],
        out_shape=[
            jax.ShapeDtypeStruct((bt, seq // ph, ph * d), f32),
            jax.ShapeDtypeStruct((bt, SSM_CONV - 1, conv_dim), f32),
            jax.ShapeDtypeStruct((bt, N_SSM_GROUPS, D_STATE, GROUP_W), f32),
        ],
        scratch_shapes=[
            pltpu.VMEM((conv_dim // cc, nb, front, cc), f32),
            pltpu.VMEM((2, nb, ph, SUBLANES + r, cc), f32),
            pltpu.VMEM((t, d_inner), f32),
            pltpu.VMEM((t, N_SSM_GROUPS * D_STATE), f32),
            pltpu.VMEM((t, N_SSM_GROUPS * D_STATE), f32),
            pltpu.VMEM((t, LANES), f32),
            pltpu.VMEM((t, LANES), f32),
            pltpu.VMEM((t, N_SSM_GROUPS * qw), f32),
            pltpu.VMEM((t, d_inner) if q != HEAD_DIM else (SUBLANES, LANES), f32),
            pltpu.VMEM((t, d_inner), f32),
            pltpu.VMEM((t, d_inner), f32),
            pltpu.VMEM((t, d_inner), bf16),
        ],
        compiler_params=pltpu.CompilerParams(
            dimension_semantics=("arbitrary", "arbitrary"), vmem_limit_bytes=VMEM_LIMIT_BYTES),
        name="ssd_mixer",
    )(x, hist_rep, s0, wz, wxbc, wdt3, conv_w, conv_b.reshape(1, conv_dim),
      dtb3, alog3, dskip, norm_w.reshape(1, d_inner), out_proj.astype(bf16), g.reshape(1, d),
      b.reshape(1, d), tri, eq, ep)
    newstate = newstate.reshape(bt, N_SSM_GROUPS, D_STATE, HEADS_PER_GROUP, HEAD_DIM)
    newstate = newstate.transpose(0, 1, 3, 4, 2).reshape(bt, N_SSM_HEADS, HEAD_DIM, D_STATE)
    return out, newhist, newstate


def _trunk(x, p, pool_hist, ssm_conv_hist, ssm_state, ffn_hist, w, *, pos0, q, pool_blk, ssd_blk, ffn_blk):
    ph = q // SUBLANES
    assert ffn_blk[2] == ph
    x, new_pool = _pool_layer(x, pool_hist[0], w['pool_w'][0], w['pool_scale'][0], w['ln_mix_g'][0],
                              w['ln_mix_b'][0], pos0=pos0, nb=pool_blk[0], tl=pool_blk[1], ph=ph)
    new_ffn = []
    x, fh = _ffn_layer(x, p[0], ffn_hist[0], w['ffn_up'][0], w['ffn_conv_w'][0], w['ffn_conv_b'][0],
                       w['ffn_down'][0], w['ln_ffn_g'][0], w['ln_ffn_b'][0], w['ple_gate_w'][0],
                       w['ple_gate_b'][0], w['ple_proj'][0], nb=ffn_blk[0], tl=ffn_blk[1], ph=ph,
                       time_order_out=False)
    new_ffn.append(fh)
    x, new_conv, new_state = _ssd_layer(
        x, ssm_conv_hist[0], ssm_state[0], w['ssm_in_proj'][0], w['ssm_conv_w'][0], w['ssm_conv_b'][0],
        w['ssm_dt_bias'][0], w['ssm_A_log'][0], w['ssm_D'][0], w['ssm_norm_w'][0], w['ssm_out_proj'][0],
        w['ln_mix_g'][1], w['ln_mix_b'][1], q=q, nb=ssd_blk[0], tl=ssd_blk[1])
    x, fh = _ffn_layer(x, p[1], ffn_hist[1], w['ffn_up'][1], w['ffn_conv_w'][1], w['ffn_conv_b'][1],
                       w['ffn_down'][1], w['ln_ffn_g'][1], w['ln_ffn_b'][1], w['ple_gate_w'][1],
                       w['ple_gate_b'][1], w['ple_proj'][1], nb=ffn_blk[0], tl=ffn_blk[1], ph=ph,
                       time_order_out=True)
    new_ffn.append(fh)
    return x, new_pool[None], new_conv[None], new_state[None], jnp.stack(new_ffn)


def _phases(tl):
    ph = SUBLANES
    while tl % (ph * SUBLANES):
        ph //= 2
    return ph


def _block_len(seq, target):
    tl = min(seq, target)
    while seq % tl:
        tl //= 2
    return tl


def kernel(x_prompt, x_sample, p_prompt, p_sample, cache_pool, cache_ssm_conv, state_ssm, cache_ffn_conv,
           pool_w, pool_scale, ssm_in_proj, ssm_conv_w, ssm_conv_b, ssm_dt_bias, ssm_A_log, ssm_D, ssm_norm_w,
           ssm_out_proj, ln_mix_g, ln_mix_b, ffn_up, ffn_conv_w, ffn_conv_b, ffn_down, ln_ffn_g, ln_ffn_b,
           ple_proj, ple_gate_w, ple_gate_b):
    assert pool_w.shape[0] == 1 and ssm_in_proj.shape[0] == 1 and ffn_up.shape[0] == DEPTH
    w = dict(pool_w=pool_w, pool_scale=pool_scale, ssm_in_proj=ssm_in_proj, ssm_conv_w=ssm_conv_w,
             ssm_conv_b=ssm_conv_b, ssm_dt_bias=ssm_dt_bias, ssm_A_log=ssm_A_log, ssm_D=ssm_D,
             ssm_norm_w=ssm_norm_w, ssm_out_proj=ssm_out_proj, ln_mix_g=ln_mix_g, ln_mix_b=ln_mix_b,
             ffn_up=ffn_up, ffn_conv_w=ffn_conv_w, ffn_conv_b=ffn_conv_b, ffn_down=ffn_down,
             ln_ffn_g=ln_ffn_g, ln_ffn_b=ln_ffn_b, ple_proj=ple_proj, ple_gate_w=ple_gate_w,
             ple_gate_b=ple_gate_b)
    bp, seq, d = x_prompt.shape
    bs, dseq, _ = x_sample.shape
    f32 = x_prompt.dtype
    conv_dim = ssm_conv_w.shape[-1]
    z_pool = jnp.zeros((1, bp, POOL_HIST, d), f32)
    z_sconv = jnp.zeros((1, bp, SSM_CONV - 1, conv_dim), f32)
    z_state = jnp.zeros((1, bp, N_SSM_HEADS, HEAD_DIM, D_STATE), f32)
    z_ffn = jnp.zeros((DEPTH, bp, FFN_CONV - 1, ffn_conv_w.shape[-1]), f32)
    tl_p = _block_len(seq, 512)
    tl_s = _block_len(seq, 256)
    y_p, pool_p, sconv_p, state_p, ffn_p = _trunk(
        x_prompt, p_prompt, z_pool, z_sconv, z_state, z_ffn, w, pos0=0, q=min(SSD_BLOCK, seq),
        pool_blk=(1, tl_p), ssd_blk=(1, tl_s), ffn_blk=(1, tl_p, _phases(tl_p)))
    y_s, pool_s, sconv_s, state_s, ffn_s = _trunk(
        x_sample, p_sample, cache_pool, cache_ssm_conv, state_ssm, cache_ffn_conv, w, pos0=PAST_LEN,
        q=dseq, pool_blk=(bs, dseq), ssd_blk=(2, dseq), ffn_blk=(bs, dseq, _phases(dseq)))
    return (y_p, y_s, pool_p, pool_s, sconv_p, sconv_s, state_p, state_s, ffn_p, ffn_s)
```

```python
import functools

import jax
import jax.numpy as jnp
from jax import lax
from jax.experimental import pallas as pl
from jax.experimental.pallas import tpu as pltpu

PAST_LEN = 1024
DEPTH = 2
ALPHA = (2 * DEPTH) ** 0.25
LN_EPS = 1e-5
RMS_EPS = 1e-5
POOL_WINDOWS = (2, 4, 8, 16)
POOL_HIST = max(POOL_WINDOWS) - 1
HEAD_DIM = 64
N_SSM_GROUPS = 8
HEADS_PER_GROUP = 4
N_SSM_HEADS = N_SSM_GROUPS * HEADS_PER_GROUP
D_STATE = 128
SSM_CONV = 4
FFN_CONV = 3
SSD_BLOCK = 64
LOG2_E = 1.4426950408889634

LANES = 128
SUBLANES = 8
VMEM_LIMIT_BYTES = 56 * 1024 * 1024

HIST_ROWS = 3 * SUBLANES
GROUP_W = HEADS_PER_GROUP * HEAD_DIM
DOWN_CHUNKS = 4
EPILOGUE_GROUPS = 2

def _bdot(a, b):
    return jnp.dot(a.astype(jnp.bfloat16), b.astype(jnp.bfloat16), preferred_element_type=jnp.float32)


def _sigmoid(v):
    return 1.0 / (1.0 + jnp.exp2(v * (-LOG2_E)))


def _silu(v):
    return v * _sigmoid(v)


def _layer_norm(v, g, b):
    mu = jnp.mean(v, axis=-1, keepdims=True)
    vc = v - mu
    var = jnp.mean(vc * vc, axis=-1, keepdims=True)
    return vc * lax.rsqrt(var + LN_EPS) * g + b


def _const_spec(shape):
    nd = len(shape)
    return pl.BlockSpec(shape, lambda b, l: (0,) * nd, pipeline_mode=pl.Buffered(1))


def _split3(v):
    hi = v.astype(jnp.bfloat16).astype(jnp.float32)
    r1 = v - hi
    mid = r1.astype(jnp.bfloat16).astype(jnp.float32)
    lo = (r1 - mid).astype(jnp.bfloat16).astype(jnp.float32)
    return hi, mid, lo


def _phase_rows(ref, ph):
    c = ref.shape[-1] // ph
    return jnp.concatenate([ref[:, :, a * c:(a + 1) * c] for a in range(ph)], axis=1)


def _store_phases(out_ref, slab_ref, v, nb, tl, ph):
    d = v.shape[-1]
    r = tl // ph
    for k in range(d // LANES):
        slab_ref[k] = v[:, k * LANES:(k + 1) * LANES]
    for n in range(nb):
        for a in range(ph):
            for k in range(d // LANES):
                out_ref[n, :, a * d + k * LANES:a * d + (k + 1) * LANES] = (
                    slab_ref[k, pl.ds(n * tl + a, r, stride=ph), :])


def _stage_time_order(slab_ref, v, a0, tl, ph):
    nb, phases, r, d = v.shape
    for n in range(nb):
        for a in range(phases):
            for k in range(d // LANES):
                slab_ref[k, pl.ds(n * tl + a0 + a, r, stride=ph), :] = v[n, a, :, k * LANES:(k + 1) * LANES]


def _store_slabs(out_ref, slab_ref, nb, tl):
    for k in range(slab_ref.shape[0]):
        out_ref[:, :, k * LANES:(k + 1) * LANES] = slab_ref[k].reshape(nb, tl, LANES)


def _phase_rows_strided(tile_refs, tl, ph):
    r = tl // ph
    return jnp.concatenate(
        [jnp.concatenate([ref[:, pl.ds(a, r, stride=ph), :] for ref in tile_refs], axis=-1)
         for a in range(ph)], axis=1)


def _phase_conv(work, hist, cw_ref, cb_ref, cols, taps):
    ph = work.shape[1]
    r = work.shape[2] - SUBLANES
    for i in range(taps - 1):
        a = ph - (taps - 1) + i
        rows = slice(i * SUBLANES, (i + 1) * SUBLANES)
        work[:, a, 0:SUBLANES, :] = hist[:, rows, :]
        hist[:, rows, :] = work[:, a, r:r + SUBLANES, :]
    outs = []
    for a in range(ph):
        c = cb_ref[:, cols]
        for k in range(taps):
            dist = taps - 1 - k
            if a >= dist:
                src = work[:, a - dist, SUBLANES:SUBLANES + r, :]
            else:
                src = work[:, a - dist + ph, SUBLANES - 1:SUBLANES - 1 + r, :]
            c = c + cw_ref[k:k + 1, cols] * src
        outs.append(c)
    return outs


def _pool_kernel(x_ref, hist_ref, w_ref, scale_ref, g_ref, b_ref, out_ref, newhist_ref, xh_ref, slab_ref,
                 lvl_ref, *, nb, tl, d, pos0, ph):
    l = pl.program_id(1)
    nl = pl.num_programs(1)
    gw = d // len(POOL_WINDOWS)
    lo, hi = SUBLANES, HIST_ROWS + tl

    @pl.when(l == 0)
    def _():
        xh_ref[:, 0:HIST_ROWS, :] = hist_ref[...]
        lvl_ref[:, :, 0:lo, :] = jnp.zeros((2, nb, lo, gw), jnp.float32)

    x = x_ref[...]
    xh_ref[:, HIST_ROWS:HIST_ROWS + tl, :] = x

    pos = pos0 + l * tl + lax.broadcasted_iota(jnp.int32, (1, tl, gw), 1)
    ys = []
    for gi, wsz in enumerate(POOL_WINDOWS):
        cols = slice(gi * gw, (gi + 1) * gw)
        s = xh_ref[:, lo:hi, cols] + xh_ref[:, lo - 1:hi - 1, cols]
        k, buf = 2, 0
        while k < wsz:
            lvl_ref[buf, :, lo:hi, :] = s
            s = lvl_ref[buf, :, lo:hi, :] + lvl_ref[buf, :, lo - k:hi - k, :]
            k, buf = 2 * k, 1 - buf
        s = s[:, HIST_ROWS - lo:, :]
        cnt = jnp.minimum(wsz, pos + 1).astype(jnp.float32)
        pooled = (s / cnt - x[:, :, cols]).reshape(nb * tl, gw)
        ys.append(_bdot(pooled, w_ref[gi]))
    y = jnp.concatenate(ys, axis=-1) * scale_ref[...]
    xf = x.reshape(nb * tl, d)
    out = _layer_norm(ALPHA * xf + y, g_ref[...], b_ref[...])
    _store_phases(out_ref, slab_ref, out, nb, tl, ph)

    @pl.when(l == nl - 1)
    def _():
        newhist_ref[...] = xh_ref[:, HIST_ROWS + tl - POOL_HIST:HIST_ROWS + tl, :]

    xh_ref[:, 0:HIST_ROWS, :] = xh_ref[:, tl:tl + HIST_ROWS, :]


def _pool_layer(x, hist, w, scale, g, b, *, pos0, nb, tl, ph):
    bt, seq, d = x.shape
    r = tl // ph
    assert bt % nb == 0 and seq % tl == 0 and tl >= HIST_ROWS and r % SUBLANES == 0
    hist16 = jnp.pad(hist, ((0, 0), (HIST_ROWS - POOL_HIST, 0), (0, 0)))
    kern = functools.partial(_pool_kernel, nb=nb, tl=tl, d=d, pos0=pos0, ph=ph)
    return pl.pallas_call(
        kern,
        grid=(bt // nb, seq // tl),
        in_specs=[
            pl.BlockSpec((nb, tl, d), lambda b, l: (b, l, 0)),
            pl.BlockSpec((nb, HIST_ROWS, d), lambda b, l: (b, 0, 0)),
            _const_spec(w.shape),
            _const_spec((1, d)), _const_spec((1, d)), _const_spec((1, d)),
        ],
        out_specs=[
            pl.BlockSpec((nb, r, ph * d), lambda b, l: (b, l, 0)),
            pl.BlockSpec((nb, POOL_HIST, d), lambda b, l: (b, 0, 0)),
        ],
        out_shape=[
            jax.ShapeDtypeStruct((bt, seq // ph, ph * d), jnp.float32),
            jax.ShapeDtypeStruct((bt, POOL_HIST, d), jnp.float32),
        ],
        scratch_shapes=[
            pltpu.VMEM((nb, HIST_ROWS + tl, d), jnp.float32),
            pltpu.VMEM((d // LANES, nb * tl, LANES), jnp.float32),
            pltpu.VMEM((2, nb, HIST_ROWS + tl, d // len(POOL_WINDOWS)), jnp.float32),
        ],
        compiler_params=pltpu.CompilerParams(
            dimension_semantics=("arbitrary", "arbitrary"), vmem_limit_bytes=VMEM_LIMIT_BYTES),
        name="pool_mixer",
    )(x, hist16, w.astype(jnp.bfloat16), scale.reshape(1, d), g.reshape(1, d), b.reshape(1, d))


def _ffn_kernel(x_ref, *refs, nb, tl, d, dff, fc, ph, p_tiles, time_order_out):
    p_refs = refs[:p_tiles]
    (hist_ref, wup_ref, cw_ref, cb_ref, wdown_ref, g_ref, b_ref, wg_ref, bg_ref, wp_ref,
     out_ref, newhist_ref, hist_s, work_s, act_s, acc_s, slab_s) = refs[p_tiles:]
    l = pl.program_id(1)
    nl = pl.num_programs(1)
    nc = dff // fc
    t = nb * tl
    r = tl // ph

    @pl.when(l == 0)
    def _():
        for j in range(2 * nc):
            hist_s[j] = hist_ref[:, :, j * fc:(j + 1) * fc]

    x = _phase_rows(x_ref, ph).reshape(t, d)
    xb = x.astype(jnp.bfloat16)

    def up_project(j):
        for half in range(2):
            work = work_s.at[2 * (j % 2) + half]
            cols = slice(half * dff + j * fc, half * dff + (j + 1) * fc)
            work[:, :, SUBLANES:, :] = _bdot(xb, wup_ref[:, cols]).reshape(nb, ph, r, fc)

    def down_project(j0, n_chunks):
        act = act_s[(j0 // DOWN_CHUNKS) % 2, :, 0:n_chunks * fc]
        part = jnp.dot(act, wdown_ref[j0 * fc:(j0 + n_chunks) * fc, :], preferred_element_type=jnp.float32)
        if j0 == 0:
            acc_s[...] = part
        else:
            acc_s[...] += part

    up_project(0)
    ready = None
    for j in range(nc):
        if j + 1 < nc:
            up_project(j + 1)
        if ready is not None:
            down_project(*ready)
            ready = None
        halves = []
        for half in range(2):
            cols = slice(half * dff + j * fc, half * dff + (j + 1) * fc)
            outs = _phase_conv(work_s.at[2 * (j % 2) + half], hist_s.at[half * nc + j], cw_ref, cb_ref,
                               cols, FFN_CONV)
            halves.append(jnp.concatenate(outs, axis=1).reshape(t, fc))
        k = j % DOWN_CHUNKS
        act_s[(j // DOWN_CHUNKS) % 2, :, k * fc:(k + 1) * fc] = (
            (_silu(halves[0]) * halves[1]).astype(jnp.bfloat16))
        if k == DOWN_CHUNKS - 1 or j == nc - 1:
            ready = (j - k, k + 1)
    down_project(*ready)

    x4 = x.reshape(nb, ph, r, d)
    acc4 = acc_s[...].reshape(nb, ph, r, d)
    p4 = _phase_rows_strided(p_refs, tl, ph).reshape(nb, ph, r, p_tiles * LANES)
    gsz = max(ph // EPILOGUE_GROUPS, 1)
    for a0 in range(0, ph, gsz):
        rows = nb * gsz * r
        pre = ALPHA * x4[:, a0:a0 + gsz] + acc4[:, a0:a0 + gsz]
        x2 = _layer_norm(pre.reshape(rows, d), g_ref[...], b_ref[...])
        gate = _sigmoid(_bdot(x2, wg_ref[...]) + bg_ref[...])
        pp = _bdot(p4[:, a0:a0 + gsz].reshape(rows, p_tiles * LANES), wp_ref[...])
        res = (x2 + gate * pp).reshape(nb, gsz, r, d)
        if time_order_out:
            _stage_time_order(slab_s, res, a0, tl, ph)
        else:
            for a in range(gsz):
                out_ref[:, :, (a0 + a) * d:(a0 + a + 1) * d] = res[:, a]
    if time_order_out:
        _store_slabs(out_ref, slab_s, nb, tl)


    @pl.when(l == nl - 1)
    def _():
        for j in range(2 * nc):
            for i in range(FFN_CONV - 1):
                row = (i + 1) * SUBLANES - 1
                newhist_ref[:, i:i + 1, j * fc:(j + 1) * fc] = hist_s[j, :, row:row + 1, :]


def _ffn_layer(x, p, hist, w_up, conv_w, conv_b, w_down, g, b, gate_w, gate_b, ple_proj, *, nb, tl, ph,
               time_order_out):
    bt, rows, phd = x.shape
    seq, d = rows * ph, phd // ph
    pd = p.shape[-1]
    dff = w_down.shape[0]
    fc = 2 * LANES
    front = (FFN_CONV - 1) * SUBLANES
    r = tl // ph
    assert bt % nb == 0 and seq % tl == 0 and tl % ph == 0 and r % SUBLANES == 0 and dff % fc == 0
    assert ph >= FFN_CONV - 1 and pd % LANES == 0 and p.shape[1] == seq
    nc = dff // fc
    p_tiles = pd // LANES
    hist_rep = jnp.repeat(hist, SUBLANES, axis=1)
    wup = w_up.astype(jnp.bfloat16)
    wdown = w_down.astype(jnp.bfloat16)
    kern = functools.partial(_ffn_kernel, nb=nb, tl=tl, d=d, dff=dff, fc=fc, ph=ph, p_tiles=p_tiles,
                             time_order_out=time_order_out)
    if time_order_out:
        out_block, out_shape, slab_rows = (nb, tl, d), (bt, seq, d), nb * tl
    else:
        out_block, out_shape, slab_rows = (nb, r, ph * d), (bt, rows, ph * d), SUBLANES
    return pl.pallas_call(
        kern,
        grid=(bt // nb, seq // tl),
        in_specs=[
            pl.BlockSpec((nb, r, ph * d), lambda b, l: (b, l, 0)),
            *[pl.BlockSpec((nb, tl, LANES), functools.partial(lambda b, l, k: (b, l, k), k=k))
              for k in range(p_tiles)],
            pl.BlockSpec((nb, front, 2 * dff), lambda b, l: (b, 0, 0)),
            _const_spec(wup.shape),
            _const_spec((FFN_CONV, 2 * dff)), _const_spec((1, 2 * dff)),
            _const_spec(wdown.shape),
            _const_spec((1, d)), _const_spec((1, d)),
            _const_spec((d, d)), _const_spec((1, d)), _const_spec((pd, d)),
        ],
        out_specs=[
            pl.BlockSpec(out_block, lambda b, l: (b, l, 0)),
            pl.BlockSpec((nb, FFN_CONV - 1, 2 * dff), lambda b, l: (b, 0, 0)),
        ],
        out_shape=[
            jax.ShapeDtypeStruct(out_shape, jnp.float32),
            jax.ShapeDtypeStruct((bt, FFN_CONV - 1, 2 * dff), jnp.float32),
        ],
        scratch_shapes=[
            pltpu.VMEM((2 * nc, nb, front, fc), jnp.float32),
            pltpu.VMEM((4, nb, ph, SUBLANES + r, fc), jnp.float32),
            pltpu.VMEM((2, nb * tl, DOWN_CHUNKS * fc), jnp.bfloat16),
            pltpu.VMEM((nb * tl, d), jnp.float32),
            pltpu.VMEM((d // LANES, slab_rows, LANES), jnp.float32),
        ],
        compiler_params=pltpu.CompilerParams(
            dimension_semantics=("arbitrary", "arbitrary"), vmem_limit_bytes=VMEM_LIMIT_BYTES),
        name="conv_ffn",
    )(x, *([p] * p_tiles), hist_rep, wup, conv_w, conv_b.reshape(1, 2 * dff), wdown, g.reshape(1, d),
      b.reshape(1, d), gate_w.astype(jnp.bfloat16), gate_b.reshape(1, d), ple_proj.astype(jnp.bfloat16))


def _ssd_kernel(x_ref, hist_ref, s0_ref, wz_ref, wxbc_ref, wdt_ref, cw_ref, cb_ref, dtb_ref, alog_ref,
                dskip_ref, nw_ref, wout_ref, g_ref, b_ref, tri_ref, eq_ref, ep_ref,
                out_ref, newhist_ref, state_ref,
                hist_s, work_s, xs_s, b_s, c_s, dt_s, acs_s, cfq_s, cfp_s, dtp_s, z_s, yb_s,
                *, nb, tl, d, q):
    l = pl.program_id(1)
    nl = pl.num_programs(1)
    t = nb * tl
    d_inner = N_SSM_HEADS * HEAD_DIM
    gn = N_SSM_GROUPS * D_STATE
    conv_dim = d_inner + 2 * gn
    ph = q // SUBLANES
    cpb = tl // q
    qw = HEADS_PER_GROUP * q
    cc = 4 * LANES
    bf16 = jnp.bfloat16

    @pl.when(l == 0)
    def _():
        hist_s[...] = hist_ref[...]
        state_ref[...] = s0_ref[...]

    tiles = [(n, c, a) for n in range(nb) for c in range(cpb) for a in range(ph)]

    def tile_rows(k):
        return slice(k * SUBLANES, (k + 1) * SUBLANES)

    x = jnp.concatenate(
        [x_ref[n, c * SUBLANES:(c + 1) * SUBLANES, a * d:(a + 1) * d] for n, c, a in tiles], axis=0)
    xb = x.astype(bf16)

    dt = _bdot(xb, wdt_ref[...]) + dtb_ref[...]
    dt = jnp.maximum(dt, 0.0) + jnp.log1p(jnp.exp(-jnp.abs(dt)))
    dt_s[...] = dt
    a_row = -jnp.exp(alog_ref[...]) * LOG2_E
    lane = lax.broadcasted_iota(jnp.int32, (q, LANES), 1)

    def lane_split(v):
        hi, mid, lo_ = _split3(v)
        return jnp.where(lane < N_SSM_HEADS, hi, jnp.where(lane < 2 * N_SSM_HEADS, mid, lo_))

    for i in range(nb * cpb):
        rows = slice(i * q, (i + 1) * q)
        a = dt_s[rows, :] * a_row
        hi, mid, lo_ = _split3(a)
        stacked = jnp.concatenate([hi, mid, lo_, jnp.zeros_like(hi)], axis=0).astype(bf16)
        acs = jnp.dot(tri_ref[...], stacked, preferred_element_type=jnp.float32)
        acs_s[rows, :] = lane_split(acs)
        dt_s[rows, :] = lane_split(dt_s[rows, :])
    acs3 = acs_s[...].astype(bf16)
    cfq_s[...] = jnp.dot(acs3, eq_ref[...], preferred_element_type=jnp.float32)
    if q != HEAD_DIM:
        cfp_s[...] = jnp.dot(acs3, ep_ref[...], preferred_element_type=jnp.float32)
    dtp_s[...] = jnp.dot(dt_s[...].astype(bf16), ep_ref[...], preferred_element_type=jnp.float32)
    cfp = cfq_s if q == HEAD_DIM else cfp_s

    def project(jc):
        xbc = _bdot(xb, wxbc_ref[:, jc * cc:(jc + 1) * cc])
        for k, (n, c, a) in enumerate(tiles):
            work_s[jc % 2, n, a, (c + 1) * SUBLANES:(c + 2) * SUBLANES, :] = xbc[tile_rows(k), :]

    project(0)
    for jc in range(conv_dim // cc):
        cols = slice(jc * cc, (jc + 1) * cc)
        if (jc + 1) * cc < conv_dim:
            project(jc + 1)
        if jc < N_SSM_GROUPS:
            z_s[:, jc * GROUP_W:(jc + 1) * GROUP_W] = _bdot(xb, wz_ref[:, jc * GROUP_W:(jc + 1) * GROUP_W])
        outs = _phase_conv(work_s.at[jc % 2], hist_s.at[jc], cw_ref, cb_ref, cols, SSM_CONV)
        outs = [_silu(v) for v in outs]
        if (jc + 1) * cc <= d_inner:
            dest, off = xs_s, jc * cc
        elif (jc + 1) * cc <= d_inner + gn:
            dest, off = b_s, jc * cc - d_inner
        else:
            dest, off = c_s, jc * cc - d_inner - gn
        for k, (n, c, a) in enumerate(tiles):
            dest[tile_rows(k), off:off + cc] = outs[a][n, c * SUBLANES:(c + 1) * SUBLANES, :]

    def chunk_time(idx):
        return ph * (idx & (SUBLANES - 1)) + (idx >> 3)

    row_q = lax.broadcasted_iota(jnp.int32, (q, qw), 0)
    lane_q = lax.broadcasted_iota(jnp.int32, (q, qw), 1) & (q - 1)
    eye_t = row_q == lane_q
    causal = chunk_time(lane_q) <= chunk_time(row_q)
    bd_mask = ((lax.broadcasted_iota(jnp.int32, (qw, GROUP_W), 0) >> (q.bit_length() - 1))
               == (lax.broadcasted_iota(jnp.int32, (qw, GROUP_W), 1) >> (HEAD_DIM.bit_length() - 1)))
    bd_ones = jnp.where(bd_mask, 1.0, 0.0).astype(bf16)
    pad_rows = LANES - q

    for i in range(nb * cpb):
        n = i // cpb
        rows = slice(i * q, (i + 1) * q)
        for g in range(N_SSM_GROUPS):
            ncols = slice(g * D_STATE, (g + 1) * D_STATE)
            pcols = slice(g * GROUP_W, (g + 1) * GROUP_W)
            bg = b_s[rows, ncols]
            cg = c_s[rows, ncols].astype(bf16)
            cf = cfq_s[rows, g * qw:(g + 1) * qw]
            cfg = cfp[rows, pcols]
            xs = xs_s[rows, pcols]
            xdt = xs * dtp_s[rows, pcols]
            rowf = jnp.sum(jnp.where(eye_t, cf, 0.0), axis=0, keepdims=True)
            lm = jnp.where(causal, jnp.exp2(cf - rowf), 0.0)
            b4 = jnp.concatenate([bg.astype(bf16)] * HEADS_PER_GROUP, axis=0)
            cb4 = lax.dot_general(cg, b4, (((1,), (1,)), ((), ())), preferred_element_type=jnp.float32)
            m = (cb4 * lm).astype(bf16)
            xbd = jnp.concatenate([xdt.astype(bf16)] * HEADS_PER_GROUP, axis=0) * bd_ones
            y_intra = jnp.dot(m, xbd, preferred_element_type=jnp.float32)
            st = state_ref[n, g]
            y_inter = jnp.dot(cg, st.astype(bf16), preferred_element_type=jnp.float32) * jnp.exp2(cfg)
            yg = y_intra + y_inter + dskip_ref[:, pcols] * xs
            yg = yg * _silu(z_s[rows, pcols])
            yg = yg * lax.rsqrt(jnp.mean(yg * yg, axis=-1, keepdims=True) + RMS_EPS)
            yb_s[rows, pcols] = (yg * nw_ref[:, pcols]).astype(bf16)
            last = cfg[q - 1:q, :]
            xd = xdt * jnp.exp2(last - cfg)
            if pad_rows:
                bgp = jnp.concatenate([bg, jnp.zeros((pad_rows, D_STATE), jnp.float32)], axis=0)
                xd = jnp.concatenate([xd, jnp.zeros((pad_rows, GROUP_W), jnp.float32)], axis=0)
            else:
                bgp = bg
            upd = jnp.dot(bgp.T.astype(bf16), xd.astype(bf16), preferred_element_type=jnp.float32)
            state_ref[n, g] = st * jnp.exp2(last) + upd

    mix = jnp.dot(yb_s[...], wout_ref[...], preferred_element_type=jnp.float32)
    res = _layer_norm(ALPHA * x + mix, g_ref[...], b_ref[...])
    for k, (n, c, a) in enumerate(tiles):
        out_ref[n, c * SUBLANES:(c + 1) * SUBLANES, a * d:(a + 1) * d] = res[tile_rows(k), :]

    @pl.when(l == nl - 1)
    def _():
        for jc in range(conv_dim // cc):
            for i in range(SSM_CONV - 1):
                row = (i + 1) * SUBLANES - 1
                newhist_ref[:, i:i + 1, jc * cc:(jc + 1) * cc] = hist_s[jc, :, row:row + 1, :]


def _expand_matrix(rep):
    k = lax.broadcasted_iota(jnp.int32, (LANES, N_SSM_HEADS * rep), 0)
    j = lax.broadcasted_iota(jnp.int32, (LANES, N_SSM_HEADS * rep), 1)
    return ((k < 3 * N_SSM_HEADS) & (k % N_SSM_HEADS == j // rep)).astype(jnp.bfloat16)


def _ssd_layer(x, conv_hist, state, in_proj, conv_w, conv_b, dt_bias, a_log, d_skip, norm_w, out_proj,
               g, b, *, q, nb, tl):
    bt, rows, phd = x.shape
    seq, d = rows * (q // SUBLANES), phd // (q // SUBLANES)
    d_inner = N_SSM_HEADS * HEAD_DIM
    conv_dim = d_inner + 2 * N_SSM_GROUPS * D_STATE
    assert bt % nb == 0 and seq % tl == 0 and tl % q == 0 and q % SUBLANES == 0 and q <= LANES
    assert q & (q - 1) == 0, "chunk length must be a power of two"
    t = nb * tl
    qw = HEADS_PER_GROUP * q
    cc = 4 * LANES
    front = (SSM_CONV - 1) * SUBLANES
    bf16 = jnp.bfloat16

    wz = in_proj[:, :d_inner].astype(bf16)
    wxbc = in_proj[:, d_inner:d_inner + conv_dim].astype(bf16)
    wdt = in_proj[:, d_inner + conv_dim:].astype(bf16)
    pad = LANES - 3 * N_SSM_HEADS
    rep3 = lambda v: jnp.pad(jnp.concatenate([v] * 3, axis=-1), ((0, 0), (0, pad)))
    wdt3 = rep3(wdt)
    dtb3 = rep3(dt_bias.reshape(1, N_SSM_HEADS))
    alog3 = rep3(a_log.reshape(1, N_SSM_HEADS))
    dskip = jnp.repeat(d_skip, HEAD_DIM).reshape(1, d_inner)
    hist_rep = jnp.repeat(conv_hist, SUBLANES, axis=1).reshape(bt, front, conv_dim // cc, cc)
    hist_rep = hist_rep.transpose(2, 0, 1, 3)
    s0 = state.reshape(bt, N_SSM_GROUPS, HEADS_PER_GROUP, HEAD_DIM, D_STATE)
    s0 = s0.transpose(0, 1, 4, 2, 3).reshape(bt, N_SSM_GROUPS, D_STATE, GROUP_W)
    ph = q // SUBLANES
    r = tl // ph
    chunk_time = lambda idx: ph * (idx % SUBLANES) + idx // SUBLANES
    tri_col = lax.broadcasted_iota(jnp.int32, (q, 4 * q), 1)
    tri = chunk_time(tri_col % q) <= chunk_time(lax.broadcasted_iota(jnp.int32, (q, 4 * q), 0))
    tri = (tri & (tri_col < 3 * q)).astype(bf16)
    eq = _expand_matrix(q)
    ep = _expand_matrix(HEAD_DIM)

    kern = functools.partial(_ssd_kernel, nb=nb, tl=tl, d=d, q=q)
    f32 = jnp.float32
    out, newhist, newstate = pl.pallas_call(
        kern,
        grid=(bt // nb, seq // tl),
        in_specs=[
            pl.BlockSpec((nb, r, ph * d), lambda b, l: (b, l, 0)),
            pl.BlockSpec((conv_dim // cc, nb, front, cc), lambda b, l: (0, b, 0, 0)),
            pl.BlockSpec((nb, N_SSM_GROUPS, D_STATE, GROUP_W), lambda b, l: (b, 0, 0, 0)),
            _const_spec(wz.shape), _const_spec(wxbc.shape), _const_spec(wdt3.shape),
            _const_spec((SSM_CONV, conv_dim)), _const_spec((1, conv_dim)),
            _const_spec((1, LANES)), _const_spec((1, LANES)),
            _const_spec((1, d_inner)), _const_spec((1, d_inner)),
            _const_spec((d_inner, d)), _const_spec((1, d)), _const_spec((1, d)),
            _const_spec(tri.shape), _const_spec(eq.shape), _const_spec(ep.shape),
        ],
        out_specs=[
            pl.BlockSpec((nb, r, ph * d), lambda b, l: (b, l, 0)),
            pl.BlockSpec((nb, SSM_CONV - 1, conv_dim), lambda b, l: (b, 0, 0)),
            pl.BlockSpec((nb, N_SSM_GROUPS, D_STATE, GROUP_W), lambda b, l: (b, 0, 0, 0)),
        ],
        out_shape=[
            jax.ShapeDtypeStruct((bt, seq // ph, ph * d), f32),
            jax.ShapeDtypeStruct((bt, SSM_CONV - 1, conv_dim), f32),
            jax.ShapeDtypeStruct((bt, N_SSM_GROUPS, D_STATE, GROUP_W), f32),
        ],
        scratch_shapes=[
            pltpu.VMEM((conv_dim // cc, nb, front, cc), f32),
            pltpu.VMEM((2, nb, ph, SUBLANES + r, cc), f32),
            pltpu.VMEM((t, d_inner), f32),
            pltpu.VMEM((t, N_SSM_GROUPS * D_STATE), f32),
            pltpu.VMEM((t, N_SSM_GROUPS * D_STATE), f32),
            pltpu.VMEM((t, LANES), f32),
            pltpu.VMEM((t, LANES), f32),
            pltpu.VMEM((t, N_SSM_GROUPS * qw), f32),
            pltpu.VMEM((t, d_inner) if q != HEAD_DIM else (SUBLANES, LANES), f32),
            pltpu.VMEM((t, d_inner), f32),
            pltpu.VMEM((t, d_inner), f32),
            pltpu.VMEM((t, d_inner), bf16),
        ],
        compiler_params=pltpu.CompilerParams(
            dimension_semantics=("arbitrary", "arbitrary"), vmem_limit_bytes=VMEM_LIMIT_BYTES),
        name="ssd_mixer",
    )(x, hist_rep, s0, wz, wxbc, wdt3, conv_w, conv_b.reshape(1, conv_dim),
      dtb3, alog3, dskip, norm_w.reshape(1, d_inner), out_proj.astype(bf16), g.reshape(1, d),
      b.reshape(1, d), tri, eq, ep)
    newstate = newstate.reshape(bt, N_SSM_GROUPS, D_STATE, HEADS_PER_GROUP, HEAD_DIM)
    newstate = newstate.transpose(0, 1, 3, 4, 2).reshape(bt, N_SSM_HEADS, HEAD_DIM, D_STATE)
    return out, newhist, newstate


def _trunk(x, p, pool_hist, ssm_conv_hist, ssm_state, ffn_hist, w, *, pos0, q, pool_blk, ssd_blk, ffn_blk):
    ph = q // SUBLANES
    assert ffn_blk[2] == ph
    x, new_pool = _pool_layer(x, pool_hist[0], w['pool_w'][0], w['pool_scale'][0], w['ln_mix_g'][0],
                              w['ln_mix_b'][0], pos0=pos0, nb=pool_blk[0], tl=pool_blk[1], ph=ph)
    new_ffn = []
    x, fh = _ffn_layer(x, p[0], ffn_hist[0], w['ffn_up'][0], w['ffn_conv_w'][0], w['ffn_conv_b'][0],
                       w['ffn_down'][0], w['ln_ffn_g'][0], w['ln_ffn_b'][0], w['ple_gate_w'][0],
                       w['ple_gate_b'][0], w['ple_proj'][0], nb=ffn_blk[0], tl=ffn_blk[1], ph=ph,
                       time_order_out=False)
    new_ffn.append(fh)
    x, new_conv, new_state = _ssd_layer(
        x, ssm_conv_hist[0], ssm_state[0], w['ssm_in_proj'][0], w['ssm_conv_w'][0], w['ssm_conv_b'][0],
        w['ssm_dt_bias'][0], w['ssm_A_log'][0], w['ssm_D'][0], w['ssm_norm_w'][0], w['ssm_out_proj'][0],
        w['ln_mix_g'][1], w['ln_mix_b'][1], q=q, nb=ssd_blk[0], tl=ssd_blk[1])
    x, fh = _ffn_layer(x, p[1], ffn_hist[1], w['ffn_up'][1], w['ffn_conv_w'][1], w['ffn_conv_b'][1],
                       w['ffn_down'][1], w['ln_ffn_g'][1], w['ln_ffn_b'][1], w['ple_gate_w'][1],
                       w['ple_gate_b'][1], w['ple_proj'][1], nb=ffn_blk[0], tl=ffn_blk[1], ph=ph,
                       time_order_out=True)
    new_ffn.append(fh)
    return x, new_pool[None], new_conv[None], new_state[None], jnp.stack(new_ffn)


def _phases(tl):
    ph = SUBLANES
    while tl % (ph * SUBLANES):
        ph //= 2
    return ph


def _block_len(seq, target):
    tl = min(seq, target)
    while seq % tl:
        tl //= 2
    return tl


def kernel(x_prompt, x_sample, p_prompt, p_sample, cache_pool, cache_ssm_conv, state_ssm, cache_ffn_conv,
           pool_w, pool_scale, ssm_in_proj, ssm_conv_w, ssm_conv_b, ssm_dt_bias, ssm_A_log, ssm_D, ssm_norm_w,
           ssm_out_proj, ln_mix_g, ln_mix_b, ffn_up, ffn_conv_w, ffn_conv_b, ffn_down, ln_ffn_g, ln_ffn_b,
           ple_proj, ple_gate_w, ple_gate_b):
    assert pool_w.shape[0] == 1 and ssm_in_proj.shape[0] == 1 and ffn_up.shape[0] == DEPTH
    w = dict(pool_w=pool_w, pool_scale=pool_scale, ssm_in_proj=ssm_in_proj, ssm_conv_w=ssm_conv_w,
             ssm_conv_b=ssm_conv_b, ssm_dt_bias=ssm_dt_bias, ssm_A_log=ssm_A_log, ssm_D=ssm_D,
             ssm_norm_w=ssm_norm_w, ssm_out_proj=ssm_out_proj, ln_mix_g=ln_mix_g, ln_mix_b=ln_mix_b,
             ffn_up=ffn_up, ffn_conv_w=ffn_conv_w, ffn_conv_b=ffn_conv_b, ffn_down=ffn_down,
             ln_ffn_g=ln_ffn_g, ln_ffn_b=ln_ffn_b, ple_proj=ple_proj, ple_gate_w=ple_gate_w,
             ple_gate_b=ple_gate_b)
    bp, seq, d = x_prompt.shape
    bs, dseq, _ = x_sample.shape
    f32 = x_prompt.dtype
    conv_dim = ssm_conv_w.shape[-1]
    z_pool = jnp.zeros((1, bp, POOL_HIST, d), f32)
    z_sconv = jnp.zeros((1, bp, SSM_CONV - 1, conv_dim), f32)
    z_state = jnp.zeros((1, bp, N_SSM_HEADS, HEAD_DIM, D_STATE), f32)
    z_ffn = jnp.zeros((DEPTH, bp, FFN_CONV - 1, ffn_conv_w.shape[-1]), f32)
    tl_p = _block_len(seq, 512)
    tl_s = _block_len(seq, 256)
    y_p, pool_p, sconv_p, state_p, ffn_p = _trunk(
        x_prompt, p_prompt, z_pool, z_sconv, z_state, z_ffn, w, pos0=0, q=min(SSD_BLOCK, seq),
        pool_blk=(1, tl_p), ssd_blk=(1, tl_s), ffn_blk=(1, tl_p, _phases(tl_p)))
    y_s, pool_s, sconv_s, state_s, ffn_s = _trunk(
        x_sample, p_sample, cache_pool, cache_ssm_conv, state_ssm, cache_ffn_conv, w, pos0=PAST_LEN,
        q=dseq, pool_blk=(bs, dseq), ssd_blk=(4 if bs % 4 == 0 else 1, dseq),
        ffn_blk=(bs, dseq, _phases(dseq)))
    return (y_p, y_s, pool_p, pool_s, sconv_p, sconv_s, state_p, state_s, ffn_p, ffn_s)
```

```python
import functools

import jax
import jax.numpy as jnp
from jax import lax
from jax.experimental import pallas as pl
from jax.experimental.pallas import tpu as pltpu

PAST_LEN = 1024
DEPTH = 2
ALPHA = (2 * DEPTH) ** 0.25
LN_EPS = 1e-5
RMS_EPS = 1e-5
POOL_WINDOWS = (2, 4, 8, 16)
POOL_HIST = max(POOL_WINDOWS) - 1
HEAD_DIM = 64
N_SSM_GROUPS = 8
HEADS_PER_GROUP = 4
N_SSM_HEADS = N_SSM_GROUPS * HEADS_PER_GROUP
D_STATE = 128
SSM_CONV = 4
FFN_CONV = 3
SSD_BLOCK = 64
LOG2_E = 1.4426950408889634

LANES = 128
SUBLANES = 8
VMEM_LIMIT_BYTES = 56 * 1024 * 1024

HIST_ROWS = 3 * SUBLANES
GROUP_W = HEADS_PER_GROUP * HEAD_DIM
DOWN_CHUNKS = 4
EPILOGUE_GROUPS = 2

def _bdot(a, b):
    return jnp.dot(a.astype(jnp.bfloat16), b.astype(jnp.bfloat16), preferred_element_type=jnp.float32)


def _sigmoid(v):
    return 1.0 / (1.0 + jnp.exp2(v * (-LOG2_E)))


def _silu(v):
    return v * _sigmoid(v)


def _layer_norm(v, g, b):
    mu = jnp.mean(v, axis=-1, keepdims=True)
    vc = v - mu
    var = jnp.mean(vc * vc, axis=-1, keepdims=True)
    return vc * lax.rsqrt(var + LN_EPS) * g + b


def _const_spec(shape):
    nd = len(shape)
    return pl.BlockSpec(shape, lambda b, l: (0,) * nd, pipeline_mode=pl.Buffered(1))


def _split3(v):
    hi = v.astype(jnp.bfloat16).astype(jnp.float32)
    r1 = v - hi
    mid = r1.astype(jnp.bfloat16).astype(jnp.float32)
    lo = (r1 - mid).astype(jnp.bfloat16).astype(jnp.float32)
    return hi, mid, lo


def _phase_rows(ref, ph):
    c = ref.shape[-1] // ph
    return jnp.concatenate([ref[:, :, a * c:(a + 1) * c] for a in range(ph)], axis=1)


def _store_phases(out_ref, slab_ref, v, nb, tl, ph):
    d = v.shape[-1]
    r = tl // ph
    for k in range(d // LANES):
        slab_ref[k] = v[:, k * LANES:(k + 1) * LANES]
    for n in range(nb):
        for a in range(ph):
            for k in range(d // LANES):
                out_ref[n, :, a * d + k * LANES:a * d + (k + 1) * LANES] = (
                    slab_ref[k, pl.ds(n * tl + a, r, stride=ph), :])


def _stage_time_order(slab_ref, v, a0, tl, ph):
    nb, phases, r, d = v.shape
    for n in range(nb):
        for a in range(phases):
            for k in range(d // LANES):
                slab_ref[k, pl.ds(n * tl + a0 + a, r, stride=ph), :] = v[n, a, :, k * LANES:(k + 1) * LANES]


def _store_slabs(out_ref, slab_ref, nb, tl):
    for k in range(slab_ref.shape[0]):
        out_ref[:, :, k * LANES:(k + 1) * LANES] = slab_ref[k].reshape(nb, tl, LANES)


def _phase_rows_strided(tile_refs, tl, ph):
    r = tl // ph
    return jnp.concatenate(
        [jnp.concatenate([ref[:, pl.ds(a, r, stride=ph), :] for ref in tile_refs], axis=-1)
         for a in range(ph)], axis=1)


def _phase_conv(work, hist, cw_ref, cb_ref, cols, taps):
    ph = work.shape[1]
    r = work.shape[2] - SUBLANES
    for i in range(taps - 1):
        a = ph - (taps - 1) + i
        rows = slice(i * SUBLANES, (i + 1) * SUBLANES)
        work[:, a, 0:SUBLANES, :] = hist[:, rows, :]
        hist[:, rows, :] = work[:, a, r:r + SUBLANES, :]
    outs = []
    for a in range(ph):
        c = cb_ref[:, cols]
        for k in range(taps):
            dist = taps - 1 - k
            if a >= dist:
                src = work[:, a - dist, SUBLANES:SUBLANES + r, :]
            else:
                src = work[:, a - dist + ph, SUBLANES - 1:SUBLANES - 1 + r, :]
            c = c + cw_ref[k:k + 1, cols] * src
        outs.append(c)
    return outs


def _pool_kernel(x_ref, hist_ref, w_ref, scale_ref, g_ref, b_ref, out_ref, newhist_ref, xh_ref, slab_ref,
                 lvl_ref, *, nb, tl, d, pos0, ph):
    l = pl.program_id(1)
    nl = pl.num_programs(1)
    gw = d // len(POOL_WINDOWS)
    lo, hi = SUBLANES, HIST_ROWS + tl

    @pl.when(l == 0)
    def _():
        xh_ref[:, 0:HIST_ROWS, :] = hist_ref[...]
        lvl_ref[:, :, 0:lo, :] = jnp.zeros((2, nb, lo, gw), jnp.float32)

    x = x_ref[...]
    xh_ref[:, HIST_ROWS:HIST_ROWS + tl, :] = x

    pos = pos0 + l * tl + lax.broadcasted_iota(jnp.int32, (1, tl, gw), 1)
    ys = []
    for gi, wsz in enumerate(POOL_WINDOWS):
        cols = slice(gi * gw, (gi + 1) * gw)
        s = xh_ref[:, lo:hi, cols] + xh_ref[:, lo - 1:hi - 1, cols]
        k, buf = 2, 0
        while k < wsz:
            lvl_ref[buf, :, lo:hi, :] = s
            s = lvl_ref[buf, :, lo:hi, :] + lvl_ref[buf, :, lo - k:hi - k, :]
            k, buf = 2 * k, 1 - buf
        s = s[:, HIST_ROWS - lo:, :]
        cnt = jnp.minimum(wsz, pos + 1).astype(jnp.float32)
        pooled = (s / cnt - x[:, :, cols]).reshape(nb * tl, gw)
        ys.append(_bdot(pooled, w_ref[gi]))
    y = jnp.concatenate(ys, axis=-1) * scale_ref[...]
    xf = x.reshape(nb * tl, d)
    out = _layer_norm(ALPHA * xf + y, g_ref[...], b_ref[...])
    _store_phases(out_ref, slab_ref, out, nb, tl, ph)

    @pl.when(l == nl - 1)
    def _():
        newhist_ref[...] = xh_ref[:, HIST_ROWS + tl - POOL_HIST:HIST_ROWS + tl, :]

    xh_ref[:, 0:HIST_ROWS, :] = xh_ref[:, tl:tl + HIST_ROWS, :]


def _pool_layer(x, hist, w, scale, g, b, *, pos0, nb, tl, ph):
    bt, seq, d = x.shape
    r = tl // ph
    assert bt % nb == 0 and seq % tl == 0 and tl >= HIST_ROWS and r % SUBLANES == 0
    hist16 = jnp.pad(hist, ((0, 0), (HIST_ROWS - POOL_HIST, 0), (0, 0)))
    kern = functools.partial(_pool_kernel, nb=nb, tl=tl, d=d, pos0=pos0, ph=ph)
    return pl.pallas_call(
        kern,
        grid=(bt // nb, seq // tl),
        in_specs=[
            pl.BlockSpec((nb, tl, d), lambda b, l: (b, l, 0)),
            pl.BlockSpec((nb, HIST_ROWS, d), lambda b, l: (b, 0, 0)),
            _const_spec(w.shape),
            _const_spec((1, d)), _const_spec((1, d)), _const_spec((1, d)),
        ],
        out_specs=[
            pl.BlockSpec((nb, r, ph * d), lambda b, l: (b, l, 0)),
            pl.BlockSpec((nb, POOL_HIST, d), lambda b, l: (b, 0, 0)),
        ],
        out_shape=[
            jax.ShapeDtypeStruct((bt, seq // ph, ph * d), jnp.float32),
            jax.ShapeDtypeStruct((bt, POOL_HIST, d), jnp.float32),
        ],
        scratch_shapes=[
            pltpu.VMEM((nb, HIST_ROWS + tl, d), jnp.float32),
            pltpu.VMEM((d // LANES, nb * tl, LANES), jnp.float32),
            pltpu.VMEM((2, nb, HIST_ROWS + tl, d // len(POOL_WINDOWS)), jnp.float32),
        ],
        compiler_params=pltpu.CompilerParams(
            dimension_semantics=("arbitrary", "arbitrary"), vmem_limit_bytes=VMEM_LIMIT_BYTES),
        name="pool_mixer",
    )(x, hist16, w.astype(jnp.bfloat16), scale.reshape(1, d), g.reshape(1, d), b.reshape(1, d))


def _ffn_kernel(x_ref, *refs, nb, tl, d, dff, fc, ph, p_tiles, time_order_out):
    p_refs = refs[:p_tiles]
    (hist_ref, wup_ref, cw_ref, cb_ref, wdown_ref, g_ref, b_ref, wg_ref, bg_ref, wp_ref,
     out_ref, newhist_ref, hist_s, work_s, act_s, acc_s, slab_s) = refs[p_tiles:]
    l = pl.program_id(1)
    nl = pl.num_programs(1)
    nc = dff // fc
    t = nb * tl
    r = tl // ph

    @pl.when(l == 0)
    def _():
        for j in range(2 * nc):
            hist_s[j] = hist_ref[:, :, j * fc:(j + 1) * fc]

    x = _phase_rows(x_ref, ph).reshape(t, d)
    xb = x.astype(jnp.bfloat16)

    def up_project(j):
        for half in range(2):
            work = work_s.at[2 * (j % 2) + half]
            cols = slice(half * dff + j * fc, half * dff + (j + 1) * fc)
            work[:, :, SUBLANES:, :] = _bdot(xb, wup_ref[:, cols]).reshape(nb, ph, r, fc)

    def down_project(j0, n_chunks):
        act = act_s[(j0 // DOWN_CHUNKS) % 2, :, 0:n_chunks * fc]
        part = jnp.dot(act, wdown_ref[j0 * fc:(j0 + n_chunks) * fc, :], preferred_element_type=jnp.float32)
        if j0 == 0:
            acc_s[...] = part
        else:
            acc_s[...] += part

    up_project(0)
    ready = None
    for j in range(nc):
        if j + 1 < nc:
            up_project(j + 1)
        if ready is not None:
            down_project(*ready)
            ready = None
        halves = []
        for half in range(2):
            cols = slice(half * dff + j * fc, half * dff + (j + 1) * fc)
            outs = _phase_conv(work_s.at[2 * (j % 2) + half], hist_s.at[half * nc + j], cw_ref, cb_ref,
                               cols, FFN_CONV)
            halves.append(jnp.concatenate(outs, axis=1).reshape(t, fc))
        k = j % DOWN_CHUNKS
        act_s[(j // DOWN_CHUNKS) % 2, :, k * fc:(k + 1) * fc] = (
            (_silu(halves[0]) * halves[1]).astype(jnp.bfloat16))
        if k == DOWN_CHUNKS - 1 or j == nc - 1:
            ready = (j - k, k + 1)
    down_project(*ready)

    x4 = x.reshape(nb, ph, r, d)
    acc4 = acc_s[...].reshape(nb, ph, r, d)
    p4 = _phase_rows_strided(p_refs, tl, ph).reshape(nb, ph, r, p_tiles * LANES)
    gsz = max(ph // EPILOGUE_GROUPS, 1)
    for a0 in range(0, ph, gsz):
        rows = nb * gsz * r
        pre = ALPHA * x4[:, a0:a0 + gsz] + acc4[:, a0:a0 + gsz]
        x2 = _layer_norm(pre.reshape(rows, d), g_ref[...], b_ref[...])
        gate = _sigmoid(_bdot(x2, wg_ref[...]) + bg_ref[...])
        pp = _bdot(p4[:, a0:a0 + gsz].reshape(rows, p_tiles * LANES), wp_ref[...])
        res = (x2 + gate * pp).reshape(nb, gsz, r, d)
        if time_order_out:
            _stage_time_order(slab_s, res, a0, tl, ph)
        else:
            for a in range(gsz):
                out_ref[:, :, (a0 + a) * d:(a0 + a + 1) * d] = res[:, a]
    if time_order_out:
        _store_slabs(out_ref, slab_s, nb, tl)


    @pl.when(l == nl - 1)
    def _():
        for j in range(2 * nc):
            for i in range(FFN_CONV - 1):
                row = (i + 1) * SUBLANES - 1
                newhist_ref[:, i:i + 1, j * fc:(j + 1) * fc] = hist_s[j, :, row:row + 1, :]


def _ffn_layer(x, p, hist, w_up, conv_w, conv_b, w_down, g, b, gate_w, gate_b, ple_proj, *, nb, tl, ph,
               time_order_out):
    bt, rows, phd = x.shape
    seq, d = rows * ph, phd // ph
    pd = p.shape[-1]
    dff = w_down.shape[0]
    fc = 2 * LANES
    front = (FFN_CONV - 1) * SUBLANES
    r = tl // ph
    assert bt % nb == 0 and seq % tl == 0 and tl % ph == 0 and r % SUBLANES == 0 and dff % fc == 0
    assert ph >= FFN_CONV - 1 and pd % LANES == 0 and p.shape[1] == seq
    nc = dff // fc
    p_tiles = pd // LANES
    hist_rep = jnp.repeat(hist, SUBLANES, axis=1)
    wup = w_up.astype(jnp.bfloat16)
    wdown = w_down.astype(jnp.bfloat16)
    kern = functools.partial(_ffn_kernel, nb=nb, tl=tl, d=d, dff=dff, fc=fc, ph=ph, p_tiles=p_tiles,
                             time_order_out=time_order_out)
    if time_order_out:
        out_block, out_shape, slab_rows = (nb, tl, d), (bt, seq, d), nb * tl
    else:
        out_block, out_shape, slab_rows = (nb, r, ph * d), (bt, rows, ph * d), SUBLANES
    return pl.pallas_call(
        kern,
        grid=(bt // nb, seq // tl),
        in_specs=[
            pl.BlockSpec((nb, r, ph * d), lambda b, l: (b, l, 0)),
            *[pl.BlockSpec((nb, tl, LANES), functools.partial(lambda b, l, k: (b, l, k), k=k))
              for k in range(p_tiles)],
            pl.BlockSpec((nb, front, 2 * dff), lambda b, l: (b, 0, 0)),
            _const_spec(wup.shape),
            _const_spec((FFN_CONV, 2 * dff)), _const_spec((1, 2 * dff)),
            _const_spec(wdown.shape),
            _const_spec((1, d)), _const_spec((1, d)),
            _const_spec((d, d)), _const_spec((1, d)), _const_spec((pd, d)),
        ],
        out_specs=[
            pl.BlockSpec(out_block, lambda b, l: (b, l, 0)),
            pl.BlockSpec((nb, FFN_CONV - 1, 2 * dff), lambda b, l: (b, 0, 0)),
        ],
        out_shape=[
            jax.ShapeDtypeStruct(out_shape, jnp.float32),
            jax.ShapeDtypeStruct((bt, FFN_CONV - 1, 2 * dff), jnp.float32),
        ],
        scratch_shapes=[
            pltpu.VMEM((2 * nc, nb, front, fc), jnp.float32),
            pltpu.VMEM((4, nb, ph, SUBLANES + r, fc), jnp.float32),
            pltpu.VMEM((2, nb * tl, DOWN_CHUNKS * fc), jnp.bfloat16),
            pltpu.VMEM((nb * tl, d), jnp.float32),
            pltpu.VMEM((d // LANES, slab_rows, LANES), jnp.float32),
        ],
        compiler_params=pltpu.CompilerParams(
            dimension_semantics=("arbitrary", "arbitrary"), vmem_limit_bytes=VMEM_LIMIT_BYTES),
        name="conv_ffn",
    )(x, *([p] * p_tiles), hist_rep, wup, conv_w, conv_b.reshape(1, 2 * dff), wdown, g.reshape(1, d),
      b.reshape(1, d), gate_w.astype(jnp.bfloat16), gate_b.reshape(1, d), ple_proj.astype(jnp.bfloat16))


def _ssd_kernel(x_ref, hist_ref, s0_ref, win_ref, wdt_ref, cw_ref, cb_ref, dtb_ref, alog_ref,
                dskip_ref, nw_ref, wout_ref, g_ref, b_ref, tri_ref, eq_ref, ep_ref,
                out_ref, newhist_ref, state_ref,
                hist_s, work_s, xs_s, b_s, c_s, dt_s, acs_s, cfq_s, cfp_s, dtp_s, z_s, yb_s,
                *, nb, tl, d, q):
    l = pl.program_id(1)
    nl = pl.num_programs(1)
    t = nb * tl
    d_inner = N_SSM_HEADS * HEAD_DIM
    gn = N_SSM_GROUPS * D_STATE
    conv_dim = d_inner + 2 * gn
    ph = q // SUBLANES
    cpb = tl // q
    qw = HEADS_PER_GROUP * q
    cc = 4 * LANES
    bf16 = jnp.bfloat16

    @pl.when(l == 0)
    def _():
        hist_s[...] = hist_ref[...]
        state_ref[...] = s0_ref[...]

    tiles = [(n, c, a) for n in range(nb) for c in range(cpb) for a in range(ph)]

    def tile_rows(k):
        return slice(k * SUBLANES, (k + 1) * SUBLANES)

    x = jnp.concatenate(
        [x_ref[n, c * SUBLANES:(c + 1) * SUBLANES, a * d:(a + 1) * d] for n, c, a in tiles], axis=0)
    xb = x.astype(bf16)

    def decay_terms():
        dt = _bdot(xb, wdt_ref[...]) + dtb_ref[...]
        dt = jnp.maximum(dt, 0.0) + jnp.log1p(jnp.exp(-jnp.abs(dt)))
        dt_s[...] = dt
        a_row = -jnp.exp(alog_ref[...]) * LOG2_E
        lane = lax.broadcasted_iota(jnp.int32, (q, LANES), 1)

        def lane_split(v):
            hi, mid, lo_ = _split3(v)
            return jnp.where(lane < N_SSM_HEADS, hi, jnp.where(lane < 2 * N_SSM_HEADS, mid, lo_))

        for i in range(nb * cpb):
            rows = slice(i * q, (i + 1) * q)
            a = dt_s[rows, :] * a_row
            hi, mid, lo_ = _split3(a)
            stacked = jnp.concatenate([hi, mid, lo_, jnp.zeros_like(hi)], axis=0).astype(bf16)
            acs = jnp.dot(tri_ref[...], stacked, preferred_element_type=jnp.float32)
            acs_s[rows, :] = lane_split(acs)
            dt_s[rows, :] = lane_split(dt_s[rows, :])
        acs3 = acs_s[...].astype(bf16)
        cfq_s[...] = jnp.dot(acs3, eq_ref[...], preferred_element_type=jnp.float32)
        if q != HEAD_DIM:
            cfp_s[...] = jnp.dot(acs3, ep_ref[...], preferred_element_type=jnp.float32)
        dtp_s[...] = jnp.dot(dt_s[...].astype(bf16), ep_ref[...], preferred_element_type=jnp.float32)

    cfp = cfq_s if q == HEAD_DIM else cfp_s

    def project(jc):
        xbc = _bdot(xb, win_ref[:, d_inner + jc * cc:d_inner + (jc + 1) * cc])
        for k, (n, c, a) in enumerate(tiles):
            work_s[jc % 2, n, a, (c + 1) * SUBLANES:(c + 2) * SUBLANES, :] = xbc[tile_rows(k), :]

    project(0)
    for jc in range(conv_dim // cc):
        cols = slice(jc * cc, (jc + 1) * cc)
        if (jc + 1) * cc < conv_dim:
            project(jc + 1)
        if jc < N_SSM_GROUPS:
            z_s[:, jc * GROUP_W:(jc + 1) * GROUP_W] = _bdot(xb, win_ref[:, jc * GROUP_W:(jc + 1) * GROUP_W])
        if jc == 0:
            decay_terms()
        outs = _phase_conv(work_s.at[jc % 2], hist_s.at[jc], cw_ref, cb_ref, cols, SSM_CONV)
        outs = [_silu(v) for v in outs]
        if (jc + 1) * cc <= d_inner:
            dest, off = xs_s, jc * cc
        elif (jc + 1) * cc <= d_inner + gn:
            dest, off = b_s, jc * cc - d_inner
        else:
            dest, off = c_s, jc * cc - d_inner - gn
        for k, (n, c, a) in enumerate(tiles):
            dest[tile_rows(k), off:off + cc] = outs[a][n, c * SUBLANES:(c + 1) * SUBLANES, :]

    def chunk_time(idx):
        return ph * (idx & (SUBLANES - 1)) + (idx >> 3)

    row_q = lax.broadcasted_iota(jnp.int32, (q, qw), 0)
    lane_q = lax.broadcasted_iota(jnp.int32, (q, qw), 1) & (q - 1)
    eye_t = row_q == lane_q
    causal = chunk_time(lane_q) <= chunk_time(row_q)
    bd_mask = ((lax.broadcasted_iota(jnp.int32, (qw, GROUP_W), 0) >> (q.bit_length() - 1))
               == (lax.broadcasted_iota(jnp.int32, (qw, GROUP_W), 1) >> (HEAD_DIM.bit_length() - 1)))
    bd_ones = jnp.where(bd_mask, 1.0, 0.0).astype(bf16)
    pad_rows = LANES - q

    for i in range(nb * cpb):
        n = i // cpb
        rows = slice(i * q, (i + 1) * q)
        for g in range(N_SSM_GROUPS):
            ncols = slice(g * D_STATE, (g + 1) * D_STATE)
            pcols = slice(g * GROUP_W, (g + 1) * GROUP_W)
            bg = b_s[rows, ncols]
            cg = c_s[rows, ncols].astype(bf16)
            cf = cfq_s[rows, g * qw:(g + 1) * qw]
            cfg = cfp[rows, pcols]
            xs = xs_s[rows, pcols]
            xdt = xs * dtp_s[rows, pcols]
            rowf = jnp.sum(jnp.where(eye_t, cf, 0.0), axis=0, keepdims=True)
            lm = jnp.where(causal, jnp.exp2(cf - rowf), 0.0)
            b4 = jnp.concatenate([bg.astype(bf16)] * HEADS_PER_GROUP, axis=0)
            cb4 = lax.dot_general(cg, b4, (((1,), (1,)), ((), ())), preferred_element_type=jnp.float32)
            m = (cb4 * lm).astype(bf16)
            xbd = jnp.concatenate([xdt.astype(bf16)] * HEADS_PER_GROUP, axis=0) * bd_ones
            y_intra = jnp.dot(m, xbd, preferred_element_type=jnp.float32)
            st = state_ref[n, g]
            y_inter = jnp.dot(cg, st.astype(bf16), preferred_element_type=jnp.float32) * jnp.exp2(cfg)
            yg = y_intra + y_inter + dskip_ref[:, pcols] * xs
            yg = yg * _silu(z_s[rows, pcols])
            yg = yg * lax.rsqrt(jnp.mean(yg * yg, axis=-1, keepdims=True) + RMS_EPS)
            yb_s[rows, pcols] = (yg * nw_ref[:, pcols]).astype(bf16)
            last = cfg[q - 1:q, :]
            xd = xdt * jnp.exp2(last - cfg)
            if pad_rows:
                bgp = jnp.concatenate([bg, jnp.zeros((pad_rows, D_STATE), jnp.float32)], axis=0)
                xd = jnp.concatenate([xd, jnp.zeros((pad_rows, GROUP_W), jnp.float32)], axis=0)
            else:
                bgp = bg
            upd = jnp.dot(bgp.T.astype(bf16), xd.astype(bf16), preferred_element_type=jnp.float32)
            state_ref[n, g] = st * jnp.exp2(last) + upd

    mix = jnp.dot(yb_s[...], wout_ref[...], preferred_element_type=jnp.float32)
    res = _layer_norm(ALPHA * x + mix, g_ref[...], b_ref[...])
    for k, (n, c, a) in enumerate(tiles):
        out_ref[n, c * SUBLANES:(c + 1) * SUBLANES, a * d:(a + 1) * d] = res[tile_rows(k), :]

    @pl.when(l == nl - 1)
    def _():
        for jc in range(conv_dim // cc):
            for i in range(SSM_CONV - 1):
                row = (i + 1) * SUBLANES - 1
                newhist_ref[:, i:i + 1, jc * cc:(jc + 1) * cc] = hist_s[jc, :, row:row + 1, :]


def _expand_matrix(rep):
    k = lax.broadcasted_iota(jnp.int32, (LANES, N_SSM_HEADS * rep), 0)
    j = lax.broadcasted_iota(jnp.int32, (LANES, N_SSM_HEADS * rep), 1)
    return ((k < 3 * N_SSM_HEADS) & (k % N_SSM_HEADS == j // rep)).astype(jnp.bfloat16)


def _ssd_layer(x, conv_hist, state, in_proj, conv_w, conv_b, dt_bias, a_log, d_skip, norm_w, out_proj,
               g, b, *, q, nb, tl):
    bt, rows, phd = x.shape
    seq, d = rows * (q // SUBLANES), phd // (q // SUBLANES)
    d_inner = N_SSM_HEADS * HEAD_DIM
    conv_dim = d_inner + 2 * N_SSM_GROUPS * D_STATE
    assert bt % nb == 0 and seq % tl == 0 and tl % q == 0 and q % SUBLANES == 0 and q <= LANES
    assert q & (q - 1) == 0, "chunk length must be a power of two"
    t = nb * tl
    qw = HEADS_PER_GROUP * q
    cc = 4 * LANES
    front = (SSM_CONV - 1) * SUBLANES
    bf16 = jnp.bfloat16

    w_in = in_proj.astype(bf16)
    wdt = w_in[:, d_inner + conv_dim:]
    pad = LANES - 3 * N_SSM_HEADS
    rep3 = lambda v: jnp.pad(jnp.concatenate([v] * 3, axis=-1), ((0, 0), (0, pad)))
    wdt3 = rep3(wdt)
    dtb3 = rep3(dt_bias.reshape(1, N_SSM_HEADS))
    alog3 = rep3(a_log.reshape(1, N_SSM_HEADS))
    dskip = jnp.repeat(d_skip, HEAD_DIM).reshape(1, d_inner)
    hist_rep = jnp.repeat(conv_hist, SUBLANES, axis=1).reshape(bt, front, conv_dim // cc, cc)
    hist_rep = hist_rep.transpose(2, 0, 1, 3)
    s0 = state.reshape(bt, N_SSM_GROUPS, HEADS_PER_GROUP, HEAD_DIM, D_STATE)
    s0 = s0.transpose(0, 1, 4, 2, 3).reshape(bt, N_SSM_GROUPS, D_STATE, GROUP_W)
    ph = q // SUBLANES
    r = tl // ph
    chunk_time = lambda idx: ph * (idx % SUBLANES) + idx // SUBLANES
    tri_col = lax.broadcasted_iota(jnp.int32, (q, 4 * q), 1)
    tri = chunk_time(tri_col % q) <= chunk_time(lax.broadcasted_iota(jnp.int32, (q, 4 * q), 0))
    tri = (tri & (tri_col < 3 * q)).astype(bf16)
    eq = _expand_matrix(q)
    ep = _expand_matrix(HEAD_DIM)

    kern = functools.partial(_ssd_kernel, nb=nb, tl=tl, d=d, q=q)
    f32 = jnp.float32
    out, newhist, newstate = pl.pallas_call(
        kern,
        grid=(bt // nb, seq // tl),
        in_specs=[
            pl.BlockSpec((nb, r, ph * d), lambda b, l: (b, l, 0)),
            pl.BlockSpec((conv_dim // cc, nb, front, cc), lambda b, l: (0, b, 0, 0)),
            pl.BlockSpec((nb, N_SSM_GROUPS, D_STATE, GROUP_W), lambda b, l: (b, 0, 0, 0)),
            _const_spec(w_in.shape), _const_spec(wdt3.shape),
            _const_spec((SSM_CONV, conv_dim)), _const_spec((1, conv_dim)),
            _const_spec((1, LANES)), _const_spec((1, LANES)),
            _const_spec((1, d_inner)), _const_spec((1, d_inner)),
            _const_spec((d_inner, d)), _const_spec((1, d)), _const_spec((1, d)),
            _const_spec(tri.shape), _const_spec(eq.shape), _const_spec(ep.shape),
        ],
        out_specs=[
            pl.BlockSpec((nb, r, ph * d), lambda b, l: (b, l, 0)),
            pl.BlockSpec((nb, SSM_CONV - 1, conv_dim), lambda b, l: (b, 0, 0)),
            pl.BlockSpec((nb, N_SSM_GROUPS, D_STATE, GROUP_W), lambda b, l: (b, 0, 0, 0)),
        ],
        out_shape=[
            jax.ShapeDtypeStruct((bt, seq // ph, ph * d), f32),
            jax.ShapeDtypeStruct((bt, SSM_CONV - 1, conv_dim), f32),
            jax.ShapeDtypeStruct((bt, N_SSM_GROUPS, D_STATE, GROUP_W), f32),
        ],
        scratch_shapes=[
            pltpu.VMEM((conv_dim // cc, nb, front, cc), f32),
            pltpu.VMEM((2, nb, ph, SUBLANES + r, cc), f32),
            pltpu.VMEM((t, d_inner), f32),
            pltpu.VMEM((t, N_SSM_GROUPS * D_STATE), f32),
            pltpu.VMEM((t, N_SSM_GROUPS * D_STATE), f32),
            pltpu.VMEM((t, LANES), f32),
            pltpu.VMEM((t, LANES), f32),
            pltpu.VMEM((t, N_SSM_GROUPS * qw), f32),
            pltpu.VMEM((t, d_inner) if q != HEAD_DIM else (SUBLANES, LANES), f32),
            pltpu.VMEM((t, d_inner), f32),
            pltpu.VMEM((t, d_inner), f32),
            pltpu.VMEM((t, d_inner), bf16),
        ],
        compiler_params=pltpu.CompilerParams(
            dimension_semantics=("arbitrary", "arbitrary"), vmem_limit_bytes=VMEM_LIMIT_BYTES),
        name="ssd_mixer",
    )(x, hist_rep, s0, w_in, wdt3, conv_w, conv_b.reshape(1, conv_dim),
      dtb3, alog3, dskip, norm_w.reshape(1, d_inner), out_proj.astype(bf16), g.reshape(1, d),
      b.reshape(1, d), tri, eq, ep)
    newstate = newstate.reshape(bt, N_SSM_GROUPS, D_STATE, HEADS_PER_GROUP, HEAD_DIM)
    newstate = newstate.transpose(0, 1, 3, 4, 2).reshape(bt, N_SSM_HEADS, HEAD_DIM, D_STATE)
    return out, newhist, newstate


def _trunk(x, p, pool_hist, ssm_conv_hist, ssm_state, ffn_hist, w, *, pos0, q, pool_blk, ssd_blk, ffn_blk):
    ph = q // SUBLANES
    assert ffn_blk[2] == ph
    x, new_pool = _pool_layer(x, pool_hist[0], w['pool_w'][0], w['pool_scale'][0], w['ln_mix_g'][0],
                              w['ln_mix_b'][0], pos0=pos0, nb=pool_blk[0], tl=pool_blk[1], ph=ph)
    new_ffn = []
    x, fh = _ffn_layer(x, p[0], ffn_hist[0], w['ffn_up'][0], w['ffn_conv_w'][0], w['ffn_conv_b'][0],
                       w['ffn_down'][0], w['ln_ffn_g'][0], w['ln_ffn_b'][0], w['ple_gate_w'][0],
                       w['ple_gate_b'][0], w['ple_proj'][0], nb=ffn_blk[0], tl=ffn_blk[1], ph=ph,
                       time_order_out=False)
    new_ffn.append(fh)
    x, new_conv, new_state = _ssd_layer(
        x, ssm_conv_hist[0], ssm_state[0], w['ssm_in_proj'][0], w['ssm_conv_w'][0], w['ssm_conv_b'][0],
        w['ssm_dt_bias'][0], w['ssm_A_log'][0], w['ssm_D'][0], w['ssm_norm_w'][0], w['ssm_out_proj'][0],
        w['ln_mix_g'][1], w['ln_mix_b'][1], q=q, nb=ssd_blk[0], tl=ssd_blk[1])
    x, fh = _ffn_layer(x, p[1], ffn_hist[1], w['ffn_up'][1], w['ffn_conv_w'][1], w['ffn_conv_b'][1],
                       w['ffn_down'][1], w['ln_ffn_g'][1], w['ln_ffn_b'][1], w['ple_gate_w'][1],
                       w['ple_gate_b'][1], w['ple_proj'][1], nb=ffn_blk[0], tl=ffn_blk[1], ph=ph,
                       time_order_out=True)
    new_ffn.append(fh)
    return x, new_pool[None], new_conv[None], new_state[None], jnp.stack(new_ffn)


def _phases(tl):
    ph = SUBLANES
    while tl % (ph * SUBLANES):
        ph //= 2
    return ph


def _block_len(seq, target):
    tl = min(seq, target)
    while seq % tl:
        tl //= 2
    return tl


def kernel(x_prompt, x_sample, p_prompt, p_sample, cache_pool, cache_ssm_conv, state_ssm, cache_ffn_conv,
           pool_w, pool_scale, ssm_in_proj, ssm_conv_w, ssm_conv_b, ssm_dt_bias, ssm_A_log, ssm_D, ssm_norm_w,
           ssm_out_proj, ln_mix_g, ln_mix_b, ffn_up, ffn_conv_w, ffn_conv_b, ffn_down, ln_ffn_g, ln_ffn_b,
           ple_proj, ple_gate_w, ple_gate_b):
    assert pool_w.shape[0] == 1 and ssm_in_proj.shape[0] == 1 and ffn_up.shape[0] == DEPTH
    w = dict(pool_w=pool_w, pool_scale=pool_scale, ssm_in_proj=ssm_in_proj, ssm_conv_w=ssm_conv_w,
             ssm_conv_b=ssm_conv_b, ssm_dt_bias=ssm_dt_bias, ssm_A_log=ssm_A_log, ssm_D=ssm_D,
             ssm_norm_w=ssm_norm_w, ssm_out_proj=ssm_out_proj, ln_mix_g=ln_mix_g, ln_mix_b=ln_mix_b,
             ffn_up=ffn_up, ffn_conv_w=ffn_conv_w, ffn_conv_b=ffn_conv_b, ffn_down=ffn_down,
             ln_ffn_g=ln_ffn_g, ln_ffn_b=ln_ffn_b, ple_proj=ple_proj, ple_gate_w=ple_gate_w,
             ple_gate_b=ple_gate_b)
    bp, seq, d = x_prompt.shape
    bs, dseq, _ = x_sample.shape
    f32 = x_prompt.dtype
    conv_dim = ssm_conv_w.shape[-1]
    z_pool = jnp.zeros((1, bp, POOL_HIST, d), f32)
    z_sconv = jnp.zeros((1, bp, SSM_CONV - 1, conv_dim), f32)
    z_state = jnp.zeros((1, bp, N_SSM_HEADS, HEAD_DIM, D_STATE), f32)
    z_ffn = jnp.zeros((DEPTH, bp, FFN_CONV - 1, ffn_conv_w.shape[-1]), f32)
    tl_p = _block_len(seq, 512)
    tl_s = _block_len(seq, 256)
    y_p, pool_p, sconv_p, state_p, ffn_p = _trunk(
        x_prompt, p_prompt, z_pool, z_sconv, z_state, z_ffn, w, pos0=0, q=min(SSD_BLOCK, seq),
        pool_blk=(1, tl_p), ssd_blk=(1, tl_s), ffn_blk=(1, tl_p, _phases(tl_p)))
    y_s, pool_s, sconv_s, state_s, ffn_s = _trunk(
        x_sample, p_sample, cache_pool, cache_ssm_conv, state_ssm, cache_ffn_conv, w, pos0=PAST_LEN,
        q=dseq, pool_blk=(bs, dseq), ssd_blk=(4 if bs % 4 == 0 else 1, dseq),
        ffn_blk=(bs, dseq, _phases(dseq)))
    return (y_p, y_s, pool_p, pool_s, sconv_p, sconv_s, state_p, state_s, ffn_p, ffn_s)
```

```python
import functools

import jax
import jax.numpy as jnp
from jax import lax
from jax.experimental import pallas as pl
from jax.experimental.pallas import tpu as pltpu

PAST_LEN = 1024
DEPTH = 2
ALPHA = (2 * DEPTH) ** 0.25
LN_EPS = 1e-5
RMS_EPS = 1e-5
POOL_WINDOWS = (2, 4, 8, 16)
POOL_HIST = max(POOL_WINDOWS) - 1
HEAD_DIM = 64
N_SSM_GROUPS = 8
HEADS_PER_GROUP = 4
N_SSM_HEADS = N_SSM_GROUPS * HEADS_PER_GROUP
D_STATE = 128
SSM_CONV = 4
FFN_CONV = 3
SSD_BLOCK = 64
LOG2_E = 1.4426950408889634

LANES = 128
SUBLANES = 8
VMEM_LIMIT_BYTES = 56 * 1024 * 1024

HIST_ROWS = 3 * SUBLANES
GROUP_W = HEADS_PER_GROUP * HEAD_DIM
DOWN_CHUNKS = 4
EPILOGUE_GROUPS = 2

def _bdot(a, b):
    return jnp.dot(a.astype(jnp.bfloat16), b.astype(jnp.bfloat16), preferred_element_type=jnp.float32)


def _sigmoid(v):
    return 1.0 / (1.0 + jnp.exp2(v * (-LOG2_E)))


def _silu(v):
    return v * _sigmoid(v)


def _layer_norm(v, g, b):
    mu = jnp.mean(v, axis=-1, keepdims=True)
    vc = v - mu
    var = jnp.mean(vc * vc, axis=-1, keepdims=True)
    return vc * lax.rsqrt(var + LN_EPS) * g + b


def _const_spec(shape):
    nd = len(shape)
    return pl.BlockSpec(shape, lambda b, l: (0,) * nd, pipeline_mode=pl.Buffered(1))


def _split3(v):
    hi = v.astype(jnp.bfloat16).astype(jnp.float32)
    r1 = v - hi
    mid = r1.astype(jnp.bfloat16).astype(jnp.float32)
    lo = (r1 - mid).astype(jnp.bfloat16).astype(jnp.float32)
    return hi, mid, lo


def _phase_rows(ref, ph):
    c = ref.shape[-1] // ph
    return jnp.concatenate([ref[:, :, a * c:(a + 1) * c] for a in range(ph)], axis=1)


def _store_phases(out_ref, slab_ref, v, nb, tl, ph):
    d = v.shape[-1]
    r = tl // ph
    for k in range(d // LANES):
        slab_ref[k] = v[:, k * LANES:(k + 1) * LANES]
    for n in range(nb):
        for a in range(ph):
            for k in range(d // LANES):
                out_ref[n, :, a * d + k * LANES:a * d + (k + 1) * LANES] = (
                    slab_ref[k, pl.ds(n * tl + a, r, stride=ph), :])


def _stage_time_order(slab_ref, v, a0, tl, ph):
    nb, phases, r, d = v.shape
    for n in range(nb):
        for a in range(phases):
            for k in range(d // LANES):
                slab_ref[k, pl.ds(n * tl + a0 + a, r, stride=ph), :] = v[n, a, :, k * LANES:(k + 1) * LANES]


def _store_slabs(out_ref, slab_ref, nb, tl):
    for k in range(slab_ref.shape[0]):
        out_ref[:, :, k * LANES:(k + 1) * LANES] = slab_ref[k].reshape(nb, tl, LANES)


def _phase_rows_strided(tile_refs, tl, ph):
    r = tl // ph
    return jnp.concatenate(
        [jnp.concatenate([ref[:, pl.ds(a, r, stride=ph), :] for ref in tile_refs], axis=-1)
         for a in range(ph)], axis=1)


def _phase_conv(work, hist, cw_ref, cb_ref, cols, taps):
    ph = work.shape[1]
    r = work.shape[2] - SUBLANES
    for i in range(taps - 1):
        a = ph - (taps - 1) + i
        rows = slice(i * SUBLANES, (i + 1) * SUBLANES)
        work[:, a, 0:SUBLANES, :] = hist[:, rows, :]
        hist[:, rows, :] = work[:, a, r:r + SUBLANES, :]
    outs = []
    for a in range(ph):
        c = cb_ref[:, cols]
        for k in range(taps):
            dist = taps - 1 - k
            if a >= dist:
                src = work[:, a - dist, SUBLANES:SUBLANES + r, :]
            else:
                src = work[:, a - dist + ph, SUBLANES - 1:SUBLANES - 1 + r, :]
            c = c + cw_ref[k:k + 1, cols] * src
        outs.append(c)
    return outs


def _pool_kernel(x_ref, hist_ref, w_ref, scale_ref, g_ref, b_ref, out_ref, newhist_ref, xh_ref, slab_ref,
                 lvl_ref, *, nb, tl, d, pos0, ph):
    l = pl.program_id(1)
    nl = pl.num_programs(1)
    gw = d // len(POOL_WINDOWS)
    lo, hi = SUBLANES, HIST_ROWS + tl

    @pl.when(l == 0)
    def _():
        xh_ref[:, 0:HIST_ROWS, :] = hist_ref[...]
        lvl_ref[:, :, 0:lo, :] = jnp.zeros((2, nb, lo, gw), jnp.float32)

    x = x_ref[...]
    xh_ref[:, HIST_ROWS:HIST_ROWS + tl, :] = x

    pos = pos0 + l * tl + lax.broadcasted_iota(jnp.int32, (1, tl, gw), 1)
    ys = []
    for gi, wsz in enumerate(POOL_WINDOWS):
        cols = slice(gi * gw, (gi + 1) * gw)
        s = xh_ref[:, lo:hi, cols] + xh_ref[:, lo - 1:hi - 1, cols]
        k, buf = 2, 0
        while k < wsz:
            lvl_ref[buf, :, lo:hi, :] = s
            s = lvl_ref[buf, :, lo:hi, :] + lvl_ref[buf, :, lo - k:hi - k, :]
            k, buf = 2 * k, 1 - buf
        s = s[:, HIST_ROWS - lo:, :]
        cnt = jnp.minimum(wsz, pos + 1).astype(jnp.float32)
        pooled = (s / cnt - x[:, :, cols]).reshape(nb * tl, gw)
        ys.append(_bdot(pooled, w_ref[gi]))
    y = jnp.concatenate(ys, axis=-1) * scale_ref[...]
    xf = x.reshape(nb * tl, d)
    out = _layer_norm(ALPHA * xf + y, g_ref[...], b_ref[...])
    _store_phases(out_ref, slab_ref, out, nb, tl, ph)

    @pl.when(l == nl - 1)
    def _():
        newhist_ref[...] = xh_ref[:, HIST_ROWS + tl - POOL_HIST:HIST_ROWS + tl, :]

    xh_ref[:, 0:HIST_ROWS, :] = xh_ref[:, tl:tl + HIST_ROWS, :]


def _pool_layer(x, hist, w, scale, g, b, *, pos0, nb, tl, ph):
    bt, seq, d = x.shape
    r = tl // ph
    assert bt % nb == 0 and seq % tl == 0 and tl >= HIST_ROWS and r % SUBLANES == 0
    hist16 = jnp.pad(hist, ((0, 0), (HIST_ROWS - POOL_HIST, 0), (0, 0)))
    kern = functools.partial(_pool_kernel, nb=nb, tl=tl, d=d, pos0=pos0, ph=ph)
    return pl.pallas_call(
        kern,
        grid=(bt // nb, seq // tl),
        in_specs=[
            pl.BlockSpec((nb, tl, d), lambda b, l: (b, l, 0)),
            pl.BlockSpec((nb, HIST_ROWS, d), lambda b, l: (b, 0, 0)),
            _const_spec(w.shape),
            _const_spec((1, d)), _const_spec((1, d)), _const_spec((1, d)),
        ],
        out_specs=[
            pl.BlockSpec((nb, r, ph * d), lambda b, l: (b, l, 0)),
            pl.BlockSpec((nb, POOL_HIST, d), lambda b, l: (b, 0, 0)),
        ],
        out_shape=[
            jax.ShapeDtypeStruct((bt, seq // ph, ph * d), jnp.float32),
            jax.ShapeDtypeStruct((bt, POOL_HIST, d), jnp.float32),
        ],
        scratch_shapes=[
            pltpu.VMEM((nb, HIST_ROWS + tl, d), jnp.float32),
            pltpu.VMEM((d // LANES, nb * tl, LANES), jnp.float32),
            pltpu.VMEM((2, nb, HIST_ROWS + tl, d // len(POOL_WINDOWS)), jnp.float32),
        ],
        compiler_params=pltpu.CompilerParams(
            dimension_semantics=("arbitrary", "arbitrary"), vmem_limit_bytes=VMEM_LIMIT_BYTES),
        name="pool_mixer",
    )(x, hist16, w.astype(jnp.bfloat16), scale.reshape(1, d), g.reshape(1, d), b.reshape(1, d))


def _ffn_kernel(x_ref, *refs, nb, tl, d, dff, fc, ph, p_tiles, time_order_out):
    p_refs = refs[:p_tiles]
    (hist_ref, wup_ref, cw_ref, cb_ref, wdown_ref, g_ref, b_ref, wg_ref, bg_ref, wp_ref,
     out_ref, newhist_ref, hist_s, work_s, act_s, acc_s, slab_s) = refs[p_tiles:]
    l = pl.program_id(1)
    nl = pl.num_programs(1)
    nc = dff // fc
    t = nb * tl
    r = tl // ph

    @pl.when(l == 0)
    def _():
        for j in range(2 * nc):
            hist_s[j] = hist_ref[:, :, j * fc:(j + 1) * fc]

    x = _phase_rows(x_ref, ph).reshape(t, d)
    xb = x.astype(jnp.bfloat16)

    def up_project(j):
        for half in range(2):
            work = work_s.at[2 * (j % 2) + half]
            cols = slice(half * dff + j * fc, half * dff + (j + 1) * fc)
            work[:, :, SUBLANES:, :] = _bdot(xb, wup_ref[:, cols]).reshape(nb, ph, r, fc)

    def down_project(j0, n_chunks):
        act = act_s[(j0 // DOWN_CHUNKS) % 2, :, 0:n_chunks * fc]
        part = jnp.dot(act, wdown_ref[j0 * fc:(j0 + n_chunks) * fc, :], preferred_element_type=jnp.float32)
        if j0 == 0:
            acc_s[...] = part
        else:
            acc_s[...] += part

    up_project(0)
    ready = None
    for j in range(nc):
        if j + 1 < nc:
            up_project(j + 1)
        if ready is not None:
            down_project(*ready)
            ready = None
        halves = []
        for half in range(2):
            cols = slice(half * dff + j * fc, half * dff + (j + 1) * fc)
            outs = _phase_conv(work_s.at[2 * (j % 2) + half], hist_s.at[half * nc + j], cw_ref, cb_ref,
                               cols, FFN_CONV)
            halves.append(jnp.concatenate(outs, axis=1).reshape(t, fc))
        k = j % DOWN_CHUNKS
        act_s[(j // DOWN_CHUNKS) % 2, :, k * fc:(k + 1) * fc] = (
            (_silu(halves[0]) * halves[1]).astype(jnp.bfloat16))
        if k == DOWN_CHUNKS - 1 or j == nc - 1:
            ready = (j - k, k + 1)
    down_project(*ready)

    x4 = x.reshape(nb, ph, r, d)
    acc4 = acc_s[...].reshape(nb, ph, r, d)
    p4 = _phase_rows_strided(p_refs, tl, ph).reshape(nb, ph, r, p_tiles * LANES)
    gsz = max(ph // EPILOGUE_GROUPS, 1)
    for a0 in range(0, ph, gsz):
        rows = nb * gsz * r
        pre = ALPHA * x4[:, a0:a0 + gsz] + acc4[:, a0:a0 + gsz]
        x2 = _layer_norm(pre.reshape(rows, d), g_ref[...], b_ref[...])
        gate = _sigmoid(_bdot(x2, wg_ref[...]) + bg_ref[...])
        pp = _bdot(p4[:, a0:a0 + gsz].reshape(rows, p_tiles * LANES), wp_ref[...])
        res = (x2 + gate * pp).reshape(nb, gsz, r, d)
        if time_order_out:
            _stage_time_order(slab_s, res, a0, tl, ph)
        else:
            for a in range(gsz):
                out_ref[:, :, (a0 + a) * d:(a0 + a + 1) * d] = res[:, a]
    if time_order_out:
        _store_slabs(out_ref, slab_s, nb, tl)


    @pl.when(l == nl - 1)
    def _():
        for j in range(2 * nc):
            for i in range(FFN_CONV - 1):
                row = (i + 1) * SUBLANES - 1
                newhist_ref[:, i:i + 1, j * fc:(j + 1) * fc] = hist_s[j, :, row:row + 1, :]


def _ffn_layer(x, p, layer, hist, w_up, conv_w, conv_b, w_down, g, b, gate_w, gate_b, ple_proj, *, nb, tl,
               ph, time_order_out):
    bt, rows, phd = x.shape
    seq, d = rows * ph, phd // ph
    pd = p.shape[-1]
    dff = w_down.shape[0]
    fc = 2 * LANES
    front = (FFN_CONV - 1) * SUBLANES
    r = tl // ph
    assert bt % nb == 0 and seq % tl == 0 and tl % ph == 0 and r % SUBLANES == 0 and dff % fc == 0
    assert ph >= FFN_CONV - 1 and pd % LANES == 0 and p.shape[2] == seq
    nc = dff // fc
    p_tiles = pd // LANES
    hist_rep = jnp.repeat(hist, SUBLANES, axis=1)
    wup = w_up.astype(jnp.bfloat16)
    wdown = w_down.astype(jnp.bfloat16)
    kern = functools.partial(_ffn_kernel, nb=nb, tl=tl, d=d, dff=dff, fc=fc, ph=ph, p_tiles=p_tiles,
                             time_order_out=time_order_out)
    if time_order_out:
        out_block, out_shape, slab_rows = (nb, tl, d), (bt, seq, d), nb * tl
    else:
        out_block, out_shape, slab_rows = (nb, r, ph * d), (bt, rows, ph * d), SUBLANES
    return pl.pallas_call(
        kern,
        grid=(bt // nb, seq // tl),
        in_specs=[
            pl.BlockSpec((nb, r, ph * d), lambda b, l: (b, l, 0)),
            *[pl.BlockSpec((None, nb, tl, LANES), functools.partial(lambda b, l, k: (layer, b, l, k), k=k))
              for k in range(p_tiles)],
            pl.BlockSpec((nb, front, 2 * dff), lambda b, l: (b, 0, 0)),
            _const_spec(wup.shape),
            _const_spec((FFN_CONV, 2 * dff)), _const_spec((1, 2 * dff)),
            _const_spec(wdown.shape),
            _const_spec((1, d)), _const_spec((1, d)),
            _const_spec((d, d)), _const_spec((1, d)), _const_spec((pd, d)),
        ],
        out_specs=[
            pl.BlockSpec(out_block, lambda b, l: (b, l, 0)),
            pl.BlockSpec((nb, FFN_CONV - 1, 2 * dff), lambda b, l: (b, 0, 0)),
        ],
        out_shape=[
            jax.ShapeDtypeStruct(out_shape, jnp.float32),
            jax.ShapeDtypeStruct((bt, FFN_CONV - 1, 2 * dff), jnp.float32),
        ],
        scratch_shapes=[
            pltpu.VMEM((2 * nc, nb, front, fc), jnp.float32),
            pltpu.VMEM((4, nb, ph, SUBLANES + r, fc), jnp.float32),
            pltpu.VMEM((2, nb * tl, DOWN_CHUNKS * fc), jnp.bfloat16),
            pltpu.VMEM((nb * tl, d), jnp.float32),
            pltpu.VMEM((d // LANES, slab_rows, LANES), jnp.float32),
        ],
        compiler_params=pltpu.CompilerParams(
            dimension_semantics=("arbitrary", "arbitrary"), vmem_limit_bytes=VMEM_LIMIT_BYTES),
        name="conv_ffn",
    )(x, *([p] * p_tiles), hist_rep, wup, conv_w, conv_b.reshape(1, 2 * dff), wdown, g.reshape(1, d),
      b.reshape(1, d), gate_w.astype(jnp.bfloat16), gate_b.reshape(1, d), ple_proj.astype(jnp.bfloat16))


def _ssd_kernel(x_ref, hist_ref, s0_ref, win_ref, wdt_ref, cw_ref, cb_ref, dtb_ref, alog_ref,
                dskip_ref, nw_ref, wout_ref, g_ref, b_ref, tri_ref, eq_ref, ep_ref,
                out_ref, newhist_ref, state_ref,
                hist_s, work_s, xs_s, b_s, c_s, dt_s, acs_s, cfq_s, cfp_s, dtp_s, z_s, yb_s,
                *, nb, tl, d, q):
    l = pl.program_id(1)
    nl = pl.num_programs(1)
    t = nb * tl
    d_inner = N_SSM_HEADS * HEAD_DIM
    gn = N_SSM_GROUPS * D_STATE
    conv_dim = d_inner + 2 * gn
    ph = q // SUBLANES
    cpb = tl // q
    qw = HEADS_PER_GROUP * q
    cc = 4 * LANES
    bf16 = jnp.bfloat16

    @pl.when(l == 0)
    def _():
        hist_s[...] = hist_ref[...]
        state_ref[...] = s0_ref[...]

    tiles = [(n, c, a) for n in range(nb) for c in range(cpb) for a in range(ph)]

    def tile_rows(k):
        return slice(k * SUBLANES, (k + 1) * SUBLANES)

    x = jnp.concatenate(
        [x_ref[n, c * SUBLANES:(c + 1) * SUBLANES, a * d:(a + 1) * d] for n, c, a in tiles], axis=0)
    xb = x.astype(bf16)

    def decay_terms():
        dt = _bdot(xb, wdt_ref[...]) + dtb_ref[...]
        dt = jnp.maximum(dt, 0.0) + jnp.log1p(jnp.exp(-jnp.abs(dt)))
        dt_s[...] = dt
        a_row = -jnp.exp(alog_ref[...]) * LOG2_E
        lane = lax.broadcasted_iota(jnp.int32, (q, LANES), 1)

        def lane_split(v):
            hi, mid, lo_ = _split3(v)
            return jnp.where(lane < N_SSM_HEADS, hi, jnp.where(lane < 2 * N_SSM_HEADS, mid, lo_))

        for i in range(nb * cpb):
            rows = slice(i * q, (i + 1) * q)
            a = dt_s[rows, :] * a_row
            hi, mid, lo_ = _split3(a)
            stacked = jnp.concatenate([hi, mid, lo_, jnp.zeros_like(hi)], axis=0).astype(bf16)
            acs = jnp.dot(tri_ref[...], stacked, preferred_element_type=jnp.float32)
            acs_s[rows, :] = lane_split(acs)
            dt_s[rows, :] = lane_split(dt_s[rows, :])
        acs3 = acs_s[...].astype(bf16)
        cfq_s[...] = jnp.dot(acs3, eq_ref[...], preferred_element_type=jnp.float32)
        if q != HEAD_DIM:
            cfp_s[...] = jnp.dot(acs3, ep_ref[...], preferred_element_type=jnp.float32)
        dtp_s[...] = jnp.dot(dt_s[...].astype(bf16), ep_ref[...], preferred_element_type=jnp.float32)

    cfp = cfq_s if q == HEAD_DIM else cfp_s

    def project(jc):
        xbc = _bdot(xb, win_ref[:, d_inner + jc * cc:d_inner + (jc + 1) * cc])
        for k, (n, c, a) in enumerate(tiles):
            work_s[jc % 2, n, a, (c + 1) * SUBLANES:(c + 2) * SUBLANES, :] = xbc[tile_rows(k), :]

    project(0)
    for jc in range(conv_dim // cc):
        cols = slice(jc * cc, (jc + 1) * cc)
        if (jc + 1) * cc < conv_dim:
            project(jc + 1)
        if jc < N_SSM_GROUPS:
            z_s[:, jc * GROUP_W:(jc + 1) * GROUP_W] = _bdot(xb, win_ref[:, jc * GROUP_W:(jc + 1) * GROUP_W])
        if jc == 0:
            decay_terms()
        outs = _phase_conv(work_s.at[jc % 2], hist_s.at[jc], cw_ref, cb_ref, cols, SSM_CONV)
        outs = [_silu(v) for v in outs]
        if (jc + 1) * cc <= d_inner:
            dest, off = xs_s, jc * cc
        elif (jc + 1) * cc <= d_inner + gn:
            dest, off = b_s, jc * cc - d_inner
        else:
            dest, off = c_s, jc * cc - d_inner - gn
        for k, (n, c, a) in enumerate(tiles):
            dest[tile_rows(k), off:off + cc] = outs[a][n, c * SUBLANES:(c + 1) * SUBLANES, :]

    def chunk_time(idx):
        return ph * (idx & (SUBLANES - 1)) + (idx >> 3)

    row_q = lax.broadcasted_iota(jnp.int32, (q, qw), 0)
    lane_q = lax.broadcasted_iota(jnp.int32, (q, qw), 1) & (q - 1)
    eye_t = row_q == lane_q
    causal = chunk_time(lane_q) <= chunk_time(row_q)
    bd_mask = ((lax.broadcasted_iota(jnp.int32, (qw, GROUP_W), 0) >> (q.bit_length() - 1))
               == (lax.broadcasted_iota(jnp.int32, (qw, GROUP_W), 1) >> (HEAD_DIM.bit_length() - 1)))
    bd_ones = jnp.where(bd_mask, 1.0, 0.0).astype(bf16)
    pad_rows = LANES - q

    for i in range(nb * cpb):
        n = i // cpb
        rows = slice(i * q, (i + 1) * q)
        for g in range(N_SSM_GROUPS):
            ncols = slice(g * D_STATE, (g + 1) * D_STATE)
            pcols = slice(g * GROUP_W, (g + 1) * GROUP_W)
            bg = b_s[rows, ncols]
            cg = c_s[rows, ncols].astype(bf16)
            cf = cfq_s[rows, g * qw:(g + 1) * qw]
            cfg = cfp[rows, pcols]
            xs = xs_s[rows, pcols]
            xdt = xs * dtp_s[rows, pcols]
            rowf = jnp.sum(jnp.where(eye_t, cf, 0.0), axis=0, keepdims=True)
            lm = jnp.where(causal, jnp.exp2(cf - rowf), 0.0)
            b4 = jnp.concatenate([bg.astype(bf16)] * HEADS_PER_GROUP, axis=0)
            cb4 = lax.dot_general(cg, b4, (((1,), (1,)), ((), ())), preferred_element_type=jnp.float32)
            m = (cb4 * lm).astype(bf16)
            xbd = jnp.concatenate([xdt.astype(bf16)] * HEADS_PER_GROUP, axis=0) * bd_ones
            y_intra = jnp.dot(m, xbd, preferred_element_type=jnp.float32)
            st = state_ref[n, g]
            y_inter = jnp.dot(cg, st.astype(bf16), preferred_element_type=jnp.float32) * jnp.exp2(cfg)
            yg = y_intra + y_inter + dskip_ref[:, pcols] * xs
            yg = yg * _silu(z_s[rows, pcols])
            yg = yg * lax.rsqrt(jnp.mean(yg * yg, axis=-1, keepdims=True) + RMS_EPS)
            yb_s[rows, pcols] = (yg * nw_ref[:, pcols]).astype(bf16)
            last = cfg[q - 1:q, :]
            xd = xdt * jnp.exp2(last - cfg)
            if pad_rows:
                bgp = jnp.concatenate([bg, jnp.zeros((pad_rows, D_STATE), jnp.float32)], axis=0)
                xd = jnp.concatenate([xd, jnp.zeros((pad_rows, GROUP_W), jnp.float32)], axis=0)
            else:
                bgp = bg
            upd = jnp.dot(bgp.T.astype(bf16), xd.astype(bf16), preferred_element_type=jnp.float32)
            state_ref[n, g] = st * jnp.exp2(last) + upd

    mix = jnp.dot(yb_s[...], wout_ref[...], preferred_element_type=jnp.float32)
    res = _layer_norm(ALPHA * x + mix, g_ref[...], b_ref[...])
    for k, (n, c, a) in enumerate(tiles):
        out_ref[n, c * SUBLANES:(c + 1) * SUBLANES, a * d:(a + 1) * d] = res[tile_rows(k), :]

    @pl.when(l == nl - 1)
    def _():
        for jc in range(conv_dim // cc):
            for i in range(SSM_CONV - 1):
                row = (i + 1) * SUBLANES - 1
                newhist_ref[:, i:i + 1, jc * cc:(jc + 1) * cc] = hist_s[jc, :, row:row + 1, :]


def _expand_matrix(rep):
    k = lax.broadcasted_iota(jnp.int32, (LANES, N_SSM_HEADS * rep), 0)
    j = lax.broadcasted_iota(jnp.int32, (LANES, N_SSM_HEADS * rep), 1)
    return ((k < 3 * N_SSM_HEADS) & (k % N_SSM_HEADS == j // rep)).astype(jnp.bfloat16)


def _ssd_layer(x, conv_hist, state, in_proj, conv_w, conv_b, dt_bias, a_log, d_skip, norm_w, out_proj,
               g, b, *, q, nb, tl):
    bt, rows, phd = x.shape
    seq, d = rows * (q // SUBLANES), phd // (q // SUBLANES)
    d_inner = N_SSM_HEADS * HEAD_DIM
    conv_dim = d_inner + 2 * N_SSM_GROUPS * D_STATE
    assert bt % nb == 0 and seq % tl == 0 and tl % q == 0 and q % SUBLANES == 0 and q <= LANES
    assert q & (q - 1) == 0, "chunk length must be a power of two"
    t = nb * tl
    qw = HEADS_PER_GROUP * q
    cc = 4 * LANES
    front = (SSM_CONV - 1) * SUBLANES
    bf16 = jnp.bfloat16

    w_in = in_proj.astype(bf16)
    wdt = w_in[:, d_inner + conv_dim:]
    pad = LANES - 3 * N_SSM_HEADS
    rep3 = lambda v: jnp.pad(jnp.concatenate([v] * 3, axis=-1), ((0, 0), (0, pad)))
    wdt3 = rep3(wdt)
    dtb3 = rep3(dt_bias.reshape(1, N_SSM_HEADS))
    alog3 = rep3(a_log.reshape(1, N_SSM_HEADS))
    dskip = jnp.repeat(d_skip, HEAD_DIM).reshape(1, d_inner)
    hist_rep = jnp.repeat(conv_hist, SUBLANES, axis=1).reshape(bt, front, conv_dim // cc, cc)
    hist_rep = hist_rep.transpose(2, 0, 1, 3)
    s0 = state.reshape(bt, N_SSM_GROUPS, HEADS_PER_GROUP, HEAD_DIM, D_STATE)
    s0 = s0.transpose(0, 1, 4, 2, 3).reshape(bt, N_SSM_GROUPS, D_STATE, GROUP_W)
    ph = q // SUBLANES
    r = tl // ph
    chunk_time = lambda idx: ph * (idx % SUBLANES) + idx // SUBLANES
    tri_col = lax.broadcasted_iota(jnp.int32, (q, 4 * q), 1)
    tri = chunk_time(tri_col % q) <= chunk_time(lax.broadcasted_iota(jnp.int32, (q, 4 * q), 0))
    tri = (tri & (tri_col < 3 * q)).astype(bf16)
    eq = _expand_matrix(q)
    ep = _expand_matrix(HEAD_DIM)

    kern = functools.partial(_ssd_kernel, nb=nb, tl=tl, d=d, q=q)
    f32 = jnp.float32
    out, newhist, newstate = pl.pallas_call(
        kern,
        grid=(bt // nb, seq // tl),
        in_specs=[
            pl.BlockSpec((nb, r, ph * d), lambda b, l: (b, l, 0)),
            pl.BlockSpec((conv_dim // cc, nb, front, cc), lambda b, l: (0, b, 0, 0)),
            pl.BlockSpec((nb, N_SSM_GROUPS, D_STATE, GROUP_W), lambda b, l: (b, 0, 0, 0)),
            _const_spec(w_in.shape), _const_spec(wdt3.shape),
            _const_spec((SSM_CONV, conv_dim)), _const_spec((1, conv_dim)),
            _const_spec((1, LANES)), _const_spec((1, LANES)),
            _const_spec((1, d_inner)), _const_spec((1, d_inner)),
            _const_spec((d_inner, d)), _const_spec((1, d)), _const_spec((1, d)),
            _const_spec(tri.shape), _const_spec(eq.shape), _const_spec(ep.shape),
        ],
        out_specs=[
            pl.BlockSpec((nb, r, ph * d), lambda b, l: (b, l, 0)),
            pl.BlockSpec((nb, SSM_CONV - 1, conv_dim), lambda b, l: (b, 0, 0)),
            pl.BlockSpec((nb, N_SSM_GROUPS, D_STATE, GROUP_W), lambda b, l: (b, 0, 0, 0)),
        ],
        out_shape=[
            jax.ShapeDtypeStruct((bt, seq // ph, ph * d), f32),
            jax.ShapeDtypeStruct((bt, SSM_CONV - 1, conv_dim), f32),
            jax.ShapeDtypeStruct((bt, N_SSM_GROUPS, D_STATE, GROUP_W), f32),
        ],
        scratch_shapes=[
            pltpu.VMEM((conv_dim // cc, nb, front, cc), f32),
            pltpu.VMEM((2, nb, ph, SUBLANES + r, cc), f32),
            pltpu.VMEM((t, d_inner), f32),
            pltpu.VMEM((t, N_SSM_GROUPS * D_STATE), f32),
            pltpu.VMEM((t, N_SSM_GROUPS * D_STATE), f32),
            pltpu.VMEM((t, LANES), f32),
            pltpu.VMEM((t, LANES), f32),
            pltpu.VMEM((t, N_SSM_GROUPS * qw), f32),
            pltpu.VMEM((t, d_inner) if q != HEAD_DIM else (SUBLANES, LANES), f32),
            pltpu.VMEM((t, d_inner), f32),
            pltpu.VMEM((t, d_inner), f32),
            pltpu.VMEM((t, d_inner), bf16),
        ],
        compiler_params=pltpu.CompilerParams(
            dimension_semantics=("arbitrary", "arbitrary"), vmem_limit_bytes=VMEM_LIMIT_BYTES),
        name="ssd_mixer",
    )(x, hist_rep, s0, w_in, wdt3, conv_w, conv_b.reshape(1, conv_dim),
      dtb3, alog3, dskip, norm_w.reshape(1, d_inner), out_proj.astype(bf16), g.reshape(1, d),
      b.reshape(1, d), tri, eq, ep)
    newstate = newstate.reshape(bt, N_SSM_GROUPS, D_STATE, HEADS_PER_GROUP, HEAD_DIM)
    newstate = newstate.transpose(0, 1, 3, 4, 2).reshape(bt, N_SSM_HEADS, HEAD_DIM, D_STATE)
    return out, newhist, newstate


def _trunk(x, p, pool_hist, ssm_conv_hist, ssm_state, ffn_hist, w, *, pos0, q, pool_blk, ssd_blk, ffn_blk):
    ph = q // SUBLANES
    assert ffn_blk[2] == ph
    x, new_pool = _pool_layer(x, pool_hist[0], w['pool_w'][0], w['pool_scale'][0], w['ln_mix_g'][0],
                              w['ln_mix_b'][0], pos0=pos0, nb=pool_blk[0], tl=pool_blk[1], ph=ph)
    new_ffn = []
    x, fh = _ffn_layer(x, p, 0, ffn_hist[0], w['ffn_up'][0], w['ffn_conv_w'][0], w['ffn_conv_b'][0],
                       w['ffn_down'][0], w['ln_ffn_g'][0], w['ln_ffn_b'][0], w['ple_gate_w'][0],
                       w['ple_gate_b'][0], w['ple_proj'][0], nb=ffn_blk[0], tl=ffn_blk[1], ph=ph,
                       time_order_out=False)
    new_ffn.append(fh)
    x, new_conv, new_state = _ssd_layer(
        x, ssm_conv_hist[0], ssm_state[0], w['ssm_in_proj'][0], w['ssm_conv_w'][0], w['ssm_conv_b'][0],
        w['ssm_dt_bias'][0], w['ssm_A_log'][0], w['ssm_D'][0], w['ssm_norm_w'][0], w['ssm_out_proj'][0],
        w['ln_mix_g'][1], w['ln_mix_b'][1], q=q, nb=ssd_blk[0], tl=ssd_blk[1])
    x, fh = _ffn_layer(x, p, 1, ffn_hist[1], w['ffn_up'][1], w['ffn_conv_w'][1], w['ffn_conv_b'][1],
                       w['ffn_down'][1], w['ln_ffn_g'][1], w['ln_ffn_b'][1], w['ple_gate_w'][1],
                       w['ple_gate_b'][1], w['ple_proj'][1], nb=ffn_blk[0], tl=ffn_blk[1], ph=ph,
                       time_order_out=True)
    new_ffn.append(fh)
    return x, new_pool[None], new_conv[None], new_state[None], jnp.stack(new_ffn)


def _phases(tl):
    ph = SUBLANES
    while tl % (ph * SUBLANES):
        ph //= 2
    return ph


def _block_len(seq, target):
    tl = min(seq, target)
    while seq % tl:
        tl //= 2
    return tl


def kernel(x_prompt, x_sample, p_prompt, p_sample, cache_pool, cache_ssm_conv, state_ssm, cache_ffn_conv,
           pool_w, pool_scale, ssm_in_proj, ssm_conv_w, ssm_conv_b, ssm_dt_bias, ssm_A_log, ssm_D, ssm_norm_w,
           ssm_out_proj, ln_mix_g, ln_mix_b, ffn_up, ffn_conv_w, ffn_conv_b, ffn_down, ln_ffn_g, ln_ffn_b,
           ple_proj, ple_gate_w, ple_gate_b):
    assert pool_w.shape[0] == 1 and ssm_in_proj.shape[0] == 1 and ffn_up.shape[0] == DEPTH
    w = dict(pool_w=pool_w, pool_scale=pool_scale, ssm_in_proj=ssm_in_proj, ssm_conv_w=ssm_conv_w,
             ssm_conv_b=ssm_conv_b, ssm_dt_bias=ssm_dt_bias, ssm_A_log=ssm_A_log, ssm_D=ssm_D,
             ssm_norm_w=ssm_norm_w, ssm_out_proj=ssm_out_proj, ln_mix_g=ln_mix_g, ln_mix_b=ln_mix_b,
             ffn_up=ffn_up, ffn_conv_w=ffn_conv_w, ffn_conv_b=ffn_conv_b, ffn_down=ffn_down,
             ln_ffn_g=ln_ffn_g, ln_ffn_b=ln_ffn_b, ple_proj=ple_proj, ple_gate_w=ple_gate_w,
             ple_gate_b=ple_gate_b)
    bp, seq, d = x_prompt.shape
    bs, dseq, _ = x_sample.shape
    f32 = x_prompt.dtype
    conv_dim = ssm_conv_w.shape[-1]
    z_pool = jnp.zeros((1, bp, POOL_HIST, d), f32)
    z_sconv = jnp.zeros((1, bp, SSM_CONV - 1, conv_dim), f32)
    z_state = jnp.zeros((1, bp, N_SSM_HEADS, HEAD_DIM, D_STATE), f32)
    z_ffn = jnp.zeros((DEPTH, bp, FFN_CONV - 1, ffn_conv_w.shape[-1]), f32)
    tl_p = _block_len(seq, 512)
    tl_s = _block_len(seq, 256)
    y_p, pool_p, sconv_p, state_p, ffn_p = _trunk(
        x_prompt, p_prompt, z_pool, z_sconv, z_state, z_ffn, w, pos0=0, q=min(SSD_BLOCK, seq),
        pool_blk=(1, tl_p), ssd_blk=(1, tl_s), ffn_blk=(1, tl_p, _phases(tl_p)))
    y_s, pool_s, sconv_s, state_s, ffn_s = _trunk(
        x_sample, p_sample, cache_pool, cache_ssm_conv, state_ssm, cache_ffn_conv, w, pos0=PAST_LEN,
        q=dseq, pool_blk=(bs, dseq), ssd_blk=(4 if bs % 4 == 0 else 1, dseq),
        ffn_blk=(bs, dseq, _phases(dseq)))
    return (y_p, y_s, pool_p, pool_s, sconv_p, sconv_s, state_p, state_s, ffn_p, ffn_s)
```

```python
import functools

import jax
import jax.numpy as jnp
from jax import lax
from jax.experimental import pallas as pl
from jax.experimental.pallas import tpu as pltpu

PAST_LEN = 1024
DEPTH = 2
ALPHA = (2 * DEPTH) ** 0.25
LN_EPS = 1e-5
RMS_EPS = 1e-5
POOL_WINDOWS = (2, 4, 8, 16)
POOL_HIST = max(POOL_WINDOWS) - 1
HEAD_DIM = 64
N_SSM_GROUPS = 8
HEADS_PER_GROUP = 4
N_SSM_HEADS = N_SSM_GROUPS * HEADS_PER_GROUP
D_STATE = 128
SSM_CONV = 4
FFN_CONV = 3
SSD_BLOCK = 64
LOG2_E = 1.4426950408889634

LANES = 128
SUBLANES = 8
VMEM_LIMIT_BYTES = 56 * 1024 * 1024

HIST_ROWS = 3 * SUBLANES
GROUP_W = HEADS_PER_GROUP * HEAD_DIM
DOWN_CHUNKS = 4
EPILOGUE_GROUPS = 2

def _bdot(a, b):
    return jnp.dot(a.astype(jnp.bfloat16), b.astype(jnp.bfloat16), preferred_element_type=jnp.float32)


def _sigmoid(v):
    return 1.0 / (1.0 + jnp.exp2(v * (-LOG2_E)))


def _silu(v):
    return v * _sigmoid(v)


def _layer_norm(v, g, b):
    mu = jnp.mean(v, axis=-1, keepdims=True)
    vc = v - mu
    var = jnp.mean(vc * vc, axis=-1, keepdims=True)
    return vc * lax.rsqrt(var + LN_EPS) * g + b


def _const_spec(shape):
    nd = len(shape)
    return pl.BlockSpec(shape, lambda b, l: (0,) * nd, pipeline_mode=pl.Buffered(1))


def _split3(v):
    hi = v.astype(jnp.bfloat16).astype(jnp.float32)
    r1 = v - hi
    mid = r1.astype(jnp.bfloat16).astype(jnp.float32)
    lo = (r1 - mid).astype(jnp.bfloat16).astype(jnp.float32)
    return hi, mid, lo


def _phase_rows(ref, ph):
    c = ref.shape[-1] // ph
    return jnp.concatenate([ref[:, :, a * c:(a + 1) * c] for a in range(ph)], axis=1)


def _store_phases(out_ref, slab_ref, v, nb, tl, ph):
    d = v.shape[-1]
    r = tl // ph
    for k in range(d // LANES):
        slab_ref[k] = v[:, k * LANES:(k + 1) * LANES]
    for n in range(nb):
        for a in range(ph):
            for k in range(d // LANES):
                out_ref[n, :, a * d + k * LANES:a * d + (k + 1) * LANES] = (
                    slab_ref[k, pl.ds(n * tl + a, r, stride=ph), :])


def _stage_time_order(slab_ref, v, a0, tl, ph):
    nb, phases, r, d = v.shape
    for n in range(nb):
        for a in range(phases):
            for k in range(d // LANES):
                slab_ref[k, pl.ds(n * tl + a0 + a, r, stride=ph), :] = v[n, a, :, k * LANES:(k + 1) * LANES]


def _store_slabs(out_ref, slab_ref, nb, tl):
    for k in range(slab_ref.shape[0]):
        out_ref[:, :, k * LANES:(k + 1) * LANES] = slab_ref[k].reshape(nb, tl, LANES)


def _phase_rows_strided(tile_refs, tl, ph):
    r = tl // ph
    return jnp.concatenate(
        [jnp.concatenate([ref[:, pl.ds(a, r, stride=ph), :] for ref in tile_refs], axis=-1)
         for a in range(ph)], axis=1)


def _phase_conv(work, hist, cw_ref, cb_ref, cols, taps):
    ph = work.shape[1]
    r = work.shape[2] - SUBLANES
    for i in range(taps - 1):
        a = ph - (taps - 1) + i
        rows = slice(i * SUBLANES, (i + 1) * SUBLANES)
        work[:, a, 0:SUBLANES, :] = hist[:, rows, :]
        hist[:, rows, :] = work[:, a, r:r + SUBLANES, :]
    outs = []
    for a in range(ph):
        c = cb_ref[:, cols]
        for k in range(taps):
            dist = taps - 1 - k
            if a >= dist:
                src = work[:, a - dist, SUBLANES:SUBLANES + r, :]
            else:
                src = work[:, a - dist + ph, SUBLANES - 1:SUBLANES - 1 + r, :]
            c = c + cw_ref[k:k + 1, cols] * src
        outs.append(c)
    return outs


def _pool_kernel(x_ref, hist_ref, w_ref, scale_ref, g_ref, b_ref, out_ref, newhist_ref, xh_ref, slab_ref,
                 lvl_ref, *, nb, tl, d, pos0, ph):
    l = pl.program_id(1)
    nl = pl.num_programs(1)
    gw = d // len(POOL_WINDOWS)
    lo, hi = SUBLANES, HIST_ROWS + tl

    @pl.when(l == 0)
    def _():
        xh_ref[:, 0:HIST_ROWS, :] = hist_ref[...]
        lvl_ref[:, :, 0:lo, :] = jnp.zeros((2, nb, lo, gw), jnp.float32)

    x = x_ref[...]
    xh_ref[:, HIST_ROWS:HIST_ROWS + tl, :] = x

    pos = pos0 + l * tl + lax.broadcasted_iota(jnp.int32, (1, tl, LANES), 1)
    ys = []
    for gi, wsz in enumerate(POOL_WINDOWS):
        cols = slice(gi * gw, (gi + 1) * gw)
        s = xh_ref[:, lo:hi, cols] + xh_ref[:, lo - 1:hi - 1, cols]
        k, buf = 2, 0
        while k < wsz:
            lvl_ref[buf, :, lo:hi, :] = s
            s = lvl_ref[buf, :, lo:hi, :] + lvl_ref[buf, :, lo - k:hi - k, :]
            k, buf = 2 * k, 1 - buf
        s = s[:, HIST_ROWS - lo:, :]
        inv = 1.0 / jnp.minimum(wsz, pos + 1).astype(jnp.float32)
        inv = jnp.concatenate([inv] * (gw // LANES), axis=-1)
        pooled = (s * inv - x[:, :, cols]).reshape(nb * tl, gw)
        ys.append(_bdot(pooled, w_ref[gi]))
    y = jnp.concatenate(ys, axis=-1) * scale_ref[...]
    xf = x.reshape(nb * tl, d)
    out = _layer_norm(ALPHA * xf + y, g_ref[...], b_ref[...])
    _store_phases(out_ref, slab_ref, out, nb, tl, ph)

    @pl.when(l == nl - 1)
    def _():
        newhist_ref[...] = xh_ref[:, HIST_ROWS + tl - POOL_HIST:HIST_ROWS + tl, :]

    xh_ref[:, 0:HIST_ROWS, :] = xh_ref[:, tl:tl + HIST_ROWS, :]


def _pool_layer(x, hist, w, scale, g, b, *, pos0, nb, tl, ph):
    bt, seq, d = x.shape
    r = tl // ph
    assert bt % nb == 0 and seq % tl == 0 and tl >= HIST_ROWS and r % SUBLANES == 0
    hist16 = jnp.pad(hist, ((0, 0), (HIST_ROWS - POOL_HIST, 0), (0, 0)))
    kern = functools.partial(_pool_kernel, nb=nb, tl=tl, d=d, pos0=pos0, ph=ph)
    return pl.pallas_call(
        kern,
        grid=(bt // nb, seq // tl),
        in_specs=[
            pl.BlockSpec((nb, tl, d), lambda b, l: (b, l, 0)),
            pl.BlockSpec((nb, HIST_ROWS, d), lambda b, l: (b, 0, 0)),
            _const_spec(w.shape),
            _const_spec((1, d)), _const_spec((1, d)), _const_spec((1, d)),
        ],
        out_specs=[
            pl.BlockSpec((nb, r, ph * d), lambda b, l: (b, l, 0)),
            pl.BlockSpec((nb, POOL_HIST, d), lambda b, l: (b, 0, 0)),
        ],
        out_shape=[
            jax.ShapeDtypeStruct((bt, seq // ph, ph * d), jnp.float32),
            jax.ShapeDtypeStruct((bt, POOL_HIST, d), jnp.float32),
        ],
        scratch_shapes=[
            pltpu.VMEM((nb, HIST_ROWS + tl, d), jnp.float32),
            pltpu.VMEM((d // LANES, nb * tl, LANES), jnp.float32),
            pltpu.VMEM((2, nb, HIST_ROWS + tl, d // len(POOL_WINDOWS)), jnp.float32),
        ],
        compiler_params=pltpu.CompilerParams(
            dimension_semantics=("arbitrary", "arbitrary"), vmem_limit_bytes=VMEM_LIMIT_BYTES),
        name="pool_mixer",
    )(x, hist16, w.astype(jnp.bfloat16), scale.reshape(1, d), g.reshape(1, d), b.reshape(1, d))


def _ffn_kernel(x_ref, *refs, nb, tl, d, dff, fc, ph, p_tiles, time_order_out):
    p_refs = refs[:p_tiles]
    (hist_ref, wup_ref, cw_ref, cb_ref, wdown_ref, g_ref, b_ref, wg_ref, bg_ref, wp_ref,
     out_ref, newhist_ref, hist_s, work_s, act_s, acc_s, slab_s) = refs[p_tiles:]
    l = pl.program_id(1)
    nl = pl.num_programs(1)
    nc = dff // fc
    t = nb * tl
    r = tl // ph

    @pl.when(l == 0)
    def _():
        for j in range(2 * nc):
            hist_s[j] = hist_ref[:, :, j * fc:(j + 1) * fc]

    x = _phase_rows(x_ref, ph).reshape(t, d)
    xb = x.astype(jnp.bfloat16)

    def up_project(j):
        for half in range(2):
            work = work_s.at[2 * (j % 2) + half]
            cols = slice(half * dff + j * fc, half * dff + (j + 1) * fc)
            work[:, :, SUBLANES:, :] = _bdot(xb, wup_ref[:, cols]).reshape(nb, ph, r, fc)

    def down_project(j0, n_chunks):
        act = act_s[(j0 // DOWN_CHUNKS) % 2, :, 0:n_chunks * fc]
        part = jnp.dot(act, wdown_ref[j0 * fc:(j0 + n_chunks) * fc, :], preferred_element_type=jnp.float32)
        if j0 == 0:
            acc_s[...] = part
        else:
            acc_s[...] += part

    up_project(0)
    ready = None
    for j in range(nc):
        if j + 1 < nc:
            up_project(j + 1)
        if ready is not None:
            down_project(*ready)
            ready = None
        halves = []
        for half in range(2):
            cols = slice(half * dff + j * fc, half * dff + (j + 1) * fc)
            outs = _phase_conv(work_s.at[2 * (j % 2) + half], hist_s.at[half * nc + j], cw_ref, cb_ref,
                               cols, FFN_CONV)
            halves.append(jnp.concatenate(outs, axis=1).reshape(t, fc))
        k = j % DOWN_CHUNKS
        act_s[(j // DOWN_CHUNKS) % 2, :, k * fc:(k + 1) * fc] = (
            (_silu(halves[0]) * halves[1]).astype(jnp.bfloat16))
        if k == DOWN_CHUNKS - 1 or j == nc - 1:
            ready = (j - k, k + 1)
    down_project(*ready)

    x4 = x.reshape(nb, ph, r, d)
    acc4 = acc_s[...].reshape(nb, ph, r, d)
    p4 = _phase_rows_strided(p_refs, tl, ph).reshape(nb, ph, r, p_tiles * LANES)
    gsz = max(ph // EPILOGUE_GROUPS, 1)
    for a0 in range(0, ph, gsz):
        rows = nb * gsz * r
        pre = ALPHA * x4[:, a0:a0 + gsz] + acc4[:, a0:a0 + gsz]
        x2 = _layer_norm(pre.reshape(rows, d), g_ref[...], b_ref[...])
        gate = _sigmoid(_bdot(x2, wg_ref[...]) + bg_ref[...])
        pp = _bdot(p4[:, a0:a0 + gsz].reshape(rows, p_tiles * LANES), wp_ref[...])
        res = (x2 + gate * pp).reshape(nb, gsz, r, d)
        if time_order_out:
            _stage_time_order(slab_s, res, a0, tl, ph)
        else:
            for a in range(gsz):
                out_ref[:, :, (a0 + a) * d:(a0 + a + 1) * d] = res[:, a]
    if time_order_out:
        _store_slabs(out_ref, slab_s, nb, tl)


    @pl.when(l == nl - 1)
    def _():
        for j in range(2 * nc):
            for i in range(FFN_CONV - 1):
                row = (i + 1) * SUBLANES - 1
                newhist_ref[:, i:i + 1, j * fc:(j + 1) * fc] = hist_s[j, :, row:row + 1, :]


def _ffn_layer(x, p, layer, hist, w_up, conv_w, conv_b, w_down, g, b, gate_w, gate_b, ple_proj, *, nb, tl,
               ph, time_order_out):
    bt, rows, phd = x.shape
    seq, d = rows * ph, phd // ph
    pd = p.shape[-1]
    dff = w_down.shape[0]
    fc = 2 * LANES
    front = (FFN_CONV - 1) * SUBLANES
    r = tl // ph
    assert bt % nb == 0 and seq % tl == 0 and tl % ph == 0 and r % SUBLANES == 0 and dff % fc == 0
    assert ph >= FFN_CONV - 1 and pd % LANES == 0 and p.shape[2] == seq
    nc = dff // fc
    p_tiles = pd // LANES
    hist_rep = jnp.repeat(hist, SUBLANES, axis=1)
    wup = w_up.astype(jnp.bfloat16)
    wdown = w_down.astype(jnp.bfloat16)
    kern = functools.partial(_ffn_kernel, nb=nb, tl=tl, d=d, dff=dff, fc=fc, ph=ph, p_tiles=p_tiles,
                             time_order_out=time_order_out)
    if time_order_out:
        out_block, out_shape, slab_rows = (nb, tl, d), (bt, seq, d), nb * tl
    else:
        out_block, out_shape, slab_rows = (nb, r, ph * d), (bt, rows, ph * d), SUBLANES
    return pl.pallas_call(
        kern,
        grid=(bt // nb, seq // tl),
        in_specs=[
            pl.BlockSpec((nb, r, ph * d), lambda b, l: (b, l, 0)),
            *[pl.BlockSpec((None, nb, tl, LANES), functools.partial(lambda b, l, k: (layer, b, l, k), k=k))
              for k in range(p_tiles)],
            pl.BlockSpec((nb, front, 2 * dff), lambda b, l: (b, 0, 0)),
            _const_spec(wup.shape),
            _const_spec((FFN_CONV, 2 * dff)), _const_spec((1, 2 * dff)),
            _const_spec(wdown.shape),
            _const_spec((1, d)), _const_spec((1, d)),
            _const_spec((d, d)), _const_spec((1, d)), _const_spec((pd, d)),
        ],
        out_specs=[
            pl.BlockSpec(out_block, lambda b, l: (b, l, 0)),
            pl.BlockSpec((nb, FFN_CONV - 1, 2 * dff), lambda b, l: (b, 0, 0)),
        ],
        out_shape=[
            jax.ShapeDtypeStruct(out_shape, jnp.float32),
            jax.ShapeDtypeStruct((bt, FFN_CONV - 1, 2 * dff), jnp.float32),
        ],
        scratch_shapes=[
            pltpu.VMEM((2 * nc, nb, front, fc), jnp.float32),
            pltpu.VMEM((4, nb, ph, SUBLANES + r, fc), jnp.float32),
            pltpu.VMEM((2, nb * tl, DOWN_CHUNKS * fc), jnp.bfloat16),
            pltpu.VMEM((nb * tl, d), jnp.float32),
            pltpu.VMEM((d // LANES, slab_rows, LANES), jnp.float32),
        ],
        compiler_params=pltpu.CompilerParams(
            dimension_semantics=("arbitrary", "arbitrary"), vmem_limit_bytes=VMEM_LIMIT_BYTES),
        name="conv_ffn",
    )(x, *([p] * p_tiles), hist_rep, wup, conv_w, conv_b.reshape(1, 2 * dff), wdown, g.reshape(1, d),
      b.reshape(1, d), gate_w.astype(jnp.bfloat16), gate_b.reshape(1, d), ple_proj.astype(jnp.bfloat16))


def _ssd_kernel(x_ref, hist_ref, s0_ref, win_ref, wdt_ref, cw_ref, cb_ref, dtb_ref, alog_ref,
                dskip_ref, nw_ref, wout_ref, g_ref, b_ref, tri_ref, eq_ref, ep_ref,
                out_ref, newhist_ref, state_ref,
                hist_s, work_s, xs_s, b_s, c_s, dt_s, acs_s, cfq_s, cfp_s, dtp_s, z_s, yb_s,
                *, nb, tl, d, q):
    l = pl.program_id(1)
    nl = pl.num_programs(1)
    t = nb * tl
    d_inner = N_SSM_HEADS * HEAD_DIM
    gn = N_SSM_GROUPS * D_STATE
    conv_dim = d_inner + 2 * gn
    ph = q // SUBLANES
    cpb = tl // q
    qw = HEADS_PER_GROUP * q
    cc = 4 * LANES
    bf16 = jnp.bfloat16

    @pl.when(l == 0)
    def _():
        hist_s[...] = hist_ref[...]
        state_ref[...] = s0_ref[...]

    tiles = [(n, c, a) for n in range(nb) for c in range(cpb) for a in range(ph)]

    def tile_rows(k):
        return slice(k * SUBLANES, (k + 1) * SUBLANES)

    x = jnp.concatenate(
        [x_ref[n, c * SUBLANES:(c + 1) * SUBLANES, a * d:(a + 1) * d] for n, c, a in tiles], axis=0)
    xb = x.astype(bf16)

    def decay_terms():
        dt = _bdot(xb, wdt_ref[...]) + dtb_ref[...]
        dt = jnp.maximum(dt, 0.0) + jnp.log1p(jnp.exp(-jnp.abs(dt)))
        dt_s[...] = dt
        a_row = -jnp.exp(alog_ref[...]) * LOG2_E
        lane = lax.broadcasted_iota(jnp.int32, (q, LANES), 1)

        def lane_split(v):
            hi, mid, lo_ = _split3(v)
            return jnp.where(lane < N_SSM_HEADS, hi, jnp.where(lane < 2 * N_SSM_HEADS, mid, lo_))

        for i in range(nb * cpb):
            rows = slice(i * q, (i + 1) * q)
            a = dt_s[rows, :] * a_row
            hi, mid, lo_ = _split3(a)
            stacked = jnp.concatenate([hi, mid, lo_, jnp.zeros_like(hi)], axis=0).astype(bf16)
            acs = jnp.dot(tri_ref[...], stacked, preferred_element_type=jnp.float32)
            acs_s[rows, :] = lane_split(acs)
            dt_s[rows, :] = lane_split(dt_s[rows, :])
        acs3 = acs_s[...].astype(bf16)
        cfq_s[...] = jnp.dot(acs3, eq_ref[...], preferred_element_type=jnp.float32)
        if q != HEAD_DIM:
            cfp_s[...] = jnp.dot(acs3, ep_ref[...], preferred_element_type=jnp.float32)
        dtp_s[...] = jnp.dot(dt_s[...].astype(bf16), ep_ref[...], preferred_element_type=jnp.float32)

    cfp = cfq_s if q == HEAD_DIM else cfp_s

    def project(jc):
        xbc = _bdot(xb, win_ref[:, d_inner + jc * cc:d_inner + (jc + 1) * cc])
        for k, (n, c, a) in enumerate(tiles):
            work_s[jc % 2, n, a, (c + 1) * SUBLANES:(c + 2) * SUBLANES, :] = xbc[tile_rows(k), :]

    project(0)
    for jc in range(conv_dim // cc):
        cols = slice(jc * cc, (jc + 1) * cc)
        if (jc + 1) * cc < conv_dim:
            project(jc + 1)
        if jc < N_SSM_GROUPS:
            z_s[:, jc * GROUP_W:(jc + 1) * GROUP_W] = _bdot(xb, win_ref[:, jc * GROUP_W:(jc + 1) * GROUP_W])
        if jc == 0:
            decay_terms()
        outs = _phase_conv(work_s.at[jc % 2], hist_s.at[jc], cw_ref, cb_ref, cols, SSM_CONV)
        outs = [_silu(v) for v in outs]
        if (jc + 1) * cc <= d_inner:
            dest, off = xs_s, jc * cc
        elif (jc + 1) * cc <= d_inner + gn:
            dest, off = b_s, jc * cc - d_inner
        else:
            dest, off = c_s, jc * cc - d_inner - gn
        for k, (n, c, a) in enumerate(tiles):
            dest[tile_rows(k), off:off + cc] = outs[a][n, c * SUBLANES:(c + 1) * SUBLANES, :]

    def chunk_time(idx):
        return ph * (idx & (SUBLANES - 1)) + (idx >> 3)

    row_q = lax.broadcasted_iota(jnp.int32, (q, qw), 0)
    lane_q = lax.broadcasted_iota(jnp.int32, (q, qw), 1) & (q - 1)
    eye_t = row_q == lane_q
    causal = chunk_time(lane_q) <= chunk_time(row_q)
    bd_mask = ((lax.broadcasted_iota(jnp.int32, (qw, GROUP_W), 0) >> (q.bit_length() - 1))
               == (lax.broadcasted_iota(jnp.int32, (qw, GROUP_W), 1) >> (HEAD_DIM.bit_length() - 1)))
    bd_ones = jnp.where(bd_mask, 1.0, 0.0).astype(bf16)
    pad_rows = LANES - q

    for i in range(nb * cpb):
        n = i // cpb
        rows = slice(i * q, (i + 1) * q)
        for g in range(N_SSM_GROUPS):
            ncols = slice(g * D_STATE, (g + 1) * D_STATE)
            pcols = slice(g * GROUP_W, (g + 1) * GROUP_W)
            bg = b_s[rows, ncols]
            cg = c_s[rows, ncols].astype(bf16)
            cf = cfq_s[rows, g * qw:(g + 1) * qw]
            cfg = cfp[rows, pcols]
            xs = xs_s[rows, pcols]
            xdt = xs * dtp_s[rows, pcols]
            rowf = jnp.sum(jnp.where(eye_t, cf, 0.0), axis=0, keepdims=True)
            lm = jnp.where(causal, jnp.exp2(cf - rowf), 0.0)
            b4 = jnp.concatenate([bg.astype(bf16)] * HEADS_PER_GROUP, axis=0)
            cb4 = lax.dot_general(cg, b4, (((1,), (1,)), ((), ())), preferred_element_type=jnp.float32)
            m = (cb4 * lm).astype(bf16)
            xbd = jnp.concatenate([xdt.astype(bf16)] * HEADS_PER_GROUP, axis=0) * bd_ones
            y_intra = jnp.dot(m, xbd, preferred_element_type=jnp.float32)
            st = state_ref[n, g]
            y_inter = jnp.dot(cg, st.astype(bf16), preferred_element_type=jnp.float32) * jnp.exp2(cfg)
            yg = y_intra + y_inter + dskip_ref[:, pcols] * xs
            yg = yg * _silu(z_s[rows, pcols])
            yg = yg * lax.rsqrt(jnp.mean(yg * yg, axis=-1, keepdims=True) + RMS_EPS)
            yb_s[rows, pcols] = (yg * nw_ref[:, pcols]).astype(bf16)
            last = cfg[q - 1:q, :]
            xd = xdt * jnp.exp2(last - cfg)
            if pad_rows:
                bgp = jnp.concatenate([bg, jnp.zeros((pad_rows, D_STATE), jnp.float32)], axis=0)
                xd = jnp.concatenate([xd, jnp.zeros((pad_rows, GROUP_W), jnp.float32)], axis=0)
            else:
                bgp = bg
            upd = jnp.dot(bgp.T.astype(bf16), xd.astype(bf16), preferred_element_type=jnp.float32)
            state_ref[n, g] = st * jnp.exp2(last) + upd

    mix = jnp.dot(yb_s[...], wout_ref[...], preferred_element_type=jnp.float32)
    res = _layer_norm(ALPHA * x + mix, g_ref[...], b_ref[...])
    for k, (n, c, a) in enumerate(tiles):
        out_ref[n, c * SUBLANES:(c + 1) * SUBLANES, a * d:(a + 1) * d] = res[tile_rows(k), :]

    @pl.when(l == nl - 1)
    def _():
        for jc in range(conv_dim // cc):
            for i in range(SSM_CONV - 1):
                row = (i + 1) * SUBLANES - 1
                newhist_ref[:, i:i + 1, jc * cc:(jc + 1) * cc] = hist_s[jc, :, row:row + 1, :]


def _expand_matrix(rep):
    k = lax.broadcasted_iota(jnp.int32, (LANES, N_SSM_HEADS * rep), 0)
    j = lax.broadcasted_iota(jnp.int32, (LANES, N_SSM_HEADS * rep), 1)
    return ((k < 3 * N_SSM_HEADS) & (k % N_SSM_HEADS == j // rep)).astype(jnp.bfloat16)


def _ssd_layer(x, conv_hist, state, in_proj, conv_w, conv_b, dt_bias, a_log, d_skip, norm_w, out_proj,
               g, b, *, q, nb, tl):
    bt, rows, phd = x.shape
    seq, d = rows * (q // SUBLANES), phd // (q // SUBLANES)
    d_inner = N_SSM_HEADS * HEAD_DIM
    conv_dim = d_inner + 2 * N_SSM_GROUPS * D_STATE
    assert bt % nb == 0 and seq % tl == 0 and tl % q == 0 and q % SUBLANES == 0 and q <= LANES
    assert q & (q - 1) == 0, "chunk length must be a power of two"
    t = nb * tl
    qw = HEADS_PER_GROUP * q
    cc = 4 * LANES
    front = (SSM_CONV - 1) * SUBLANES
    bf16 = jnp.bfloat16

    w_in = in_proj.astype(bf16)
    wdt = w_in[:, d_inner + conv_dim:]
    pad = LANES - 3 * N_SSM_HEADS
    rep3 = lambda v: jnp.pad(jnp.concatenate([v] * 3, axis=-1), ((0, 0), (0, pad)))
    wdt3 = rep3(wdt)
    dtb3 = rep3(dt_bias.reshape(1, N_SSM_HEADS))
    alog3 = rep3(a_log.reshape(1, N_SSM_HEADS))
    dskip = jnp.repeat(d_skip, HEAD_DIM).reshape(1, d_inner)
    hist_rep = jnp.repeat(conv_hist, SUBLANES, axis=1).reshape(bt, front, conv_dim // cc, cc)
    hist_rep = hist_rep.transpose(2, 0, 1, 3)
    s0 = state.reshape(bt, N_SSM_GROUPS, HEADS_PER_GROUP, HEAD_DIM, D_STATE)
    s0 = s0.transpose(0, 1, 4, 2, 3).reshape(bt, N_SSM_GROUPS, D_STATE, GROUP_W)
    ph = q // SUBLANES
    r = tl // ph
    chunk_time = lambda idx: ph * (idx % SUBLANES) + idx // SUBLANES
    tri_col = lax.broadcasted_iota(jnp.int32, (q, 4 * q), 1)
    tri = chunk_time(tri_col % q) <= chunk_time(lax.broadcasted_iota(jnp.int32, (q, 4 * q), 0))
    tri = (tri & (tri_col < 3 * q)).astype(bf16)
    eq = _expand_matrix(q)
    ep = _expand_matrix(HEAD_DIM)

    kern = functools.partial(_ssd_kernel, nb=nb, tl=tl, d=d, q=q)
    f32 = jnp.float32
    out, newhist, newstate = pl.pallas_call(
        kern,
        grid=(bt // nb, seq // tl),
        in_specs=[
            pl.BlockSpec((nb, r, ph * d), lambda b, l: (b, l, 0)),
            pl.BlockSpec((conv_dim // cc, nb, front, cc), lambda b, l: (0, b, 0, 0)),
            pl.BlockSpec((nb, N_SSM_GROUPS, D_STATE, GROUP_W), lambda b, l: (b, 0, 0, 0)),
            _const_spec(w_in.shape), _const_spec(wdt3.shape),
            _const_spec((SSM_CONV, conv_dim)), _const_spec((1, conv_dim)),
            _const_spec((1, LANES)), _const_spec((1, LANES)),
            _const_spec((1, d_inner)), _const_spec((1, d_inner)),
            _const_spec((d_inner, d)), _const_spec((1, d)), _const_spec((1, d)),
            _const_spec(tri.shape), _const_spec(eq.shape), _const_spec(ep.shape),
        ],
        out_specs=[
            pl.BlockSpec((nb, r, ph * d), lambda b, l: (b, l, 0)),
            pl.BlockSpec((nb, SSM_CONV - 1, conv_dim), lambda b, l: (b, 0, 0)),
            pl.BlockSpec((nb, N_SSM_GROUPS, D_STATE, GROUP_W), lambda b, l: (b, 0, 0, 0)),
        ],
        out_shape=[
            jax.ShapeDtypeStruct((bt, seq // ph, ph * d), f32),
            jax.ShapeDtypeStruct((bt, SSM_CONV - 1, conv_dim), f32),
            jax.ShapeDtypeStruct((bt, N_SSM_GROUPS, D_STATE, GROUP_W), f32),
        ],
        scratch_shapes=[
            pltpu.VMEM((conv_dim // cc, nb, front, cc), f32),
            pltpu.VMEM((2, nb, ph, SUBLANES + r, cc), f32),
            pltpu.VMEM((t, d_inner), f32),
            pltpu.VMEM((t, N_SSM_GROUPS * D_STATE), f32),
            pltpu.VMEM((t, N_SSM_GROUPS * D_STATE), f32),
            pltpu.VMEM((t, LANES), f32),
            pltpu.VMEM((t, LANES), f32),
            pltpu.VMEM((t, N_SSM_GROUPS * qw), f32),
            pltpu.VMEM((t, d_inner) if q != HEAD_DIM else (SUBLANES, LANES), f32),
            pltpu.VMEM((t, d_inner), f32),
            pltpu.VMEM((t, d_inner), f32),
            pltpu.VMEM((t, d_inner), bf16),
        ],
        compiler_params=pltpu.CompilerParams(
            dimension_semantics=("arbitrary", "arbitrary"), vmem_limit_bytes=VMEM_LIMIT_BYTES),
        name="ssd_mixer",
    )(x, hist_rep, s0, w_in, wdt3, conv_w, conv_b.reshape(1, conv_dim),
      dtb3, alog3, dskip, norm_w.reshape(1, d_inner), out_proj.astype(bf16), g.reshape(1, d),
      b.reshape(1, d), tri, eq, ep)
    newstate = newstate.reshape(bt, N_SSM_GROUPS, D_STATE, HEADS_PER_GROUP, HEAD_DIM)
    newstate = newstate.transpose(0, 1, 3, 4, 2).reshape(bt, N_SSM_HEADS, HEAD_DIM, D_STATE)
    return out, newhist, newstate


def _trunk(x, p, pool_hist, ssm_conv_hist, ssm_state, ffn_hist, w, *, pos0, q, pool_blk, ssd_blk, ffn_blk):
    ph = q // SUBLANES
    assert ffn_blk[2] == ph
    x, new_pool = _pool_layer(x, pool_hist[0], w['pool_w'][0], w['pool_scale'][0], w['ln_mix_g'][0],
                              w['ln_mix_b'][0], pos0=pos0, nb=pool_blk[0], tl=pool_blk[1], ph=ph)
    new_ffn = []
    x, fh = _ffn_layer(x, p, 0, ffn_hist[0], w['ffn_up'][0], w['ffn_conv_w'][0], w['ffn_conv_b'][0],
                       w['ffn_down'][0], w['ln_ffn_g'][0], w['ln_ffn_b'][0], w['ple_gate_w'][0],
                       w['ple_gate_b'][0], w['ple_proj'][0], nb=ffn_blk[0], tl=ffn_blk[1], ph=ph,
                       time_order_out=False)
    new_ffn.append(fh)
    x, new_conv, new_state = _ssd_layer(
        x, ssm_conv_hist[0], ssm_state[0], w['ssm_in_proj'][0], w['ssm_conv_w'][0], w['ssm_conv_b'][0],
        w['ssm_dt_bias'][0], w['ssm_A_log'][0], w['ssm_D'][0], w['ssm_norm_w'][0], w['ssm_out_proj'][0],
        w['ln_mix_g'][1], w['ln_mix_b'][1], q=q, nb=ssd_blk[0], tl=ssd_blk[1])
    x, fh = _ffn_layer(x, p, 1, ffn_hist[1], w['ffn_up'][1], w['ffn_conv_w'][1], w['ffn_conv_b'][1],
                       w['ffn_down'][1], w['ln_ffn_g'][1], w['ln_ffn_b'][1], w['ple_gate_w'][1],
                       w['ple_gate_b'][1], w['ple_proj'][1], nb=ffn_blk[0], tl=ffn_blk[1], ph=ph,
                       time_order_out=True)
    new_ffn.append(fh)
    return x, new_pool[None], new_conv[None], new_state[None], jnp.stack(new_ffn)


def _phases(tl):
    ph = SUBLANES
    while tl % (ph * SUBLANES):
        ph //= 2
    return ph


def _block_len(seq, target):
    tl = min(seq, target)
    while seq % tl:
        tl //= 2
    return tl


def kernel(x_prompt, x_sample, p_prompt, p_sample, cache_pool, cache_ssm_conv, state_ssm, cache_ffn_conv,
           pool_w, pool_scale, ssm_in_proj, ssm_conv_w, ssm_conv_b, ssm_dt_bias, ssm_A_log, ssm_D, ssm_norm_w,
           ssm_out_proj, ln_mix_g, ln_mix_b, ffn_up, ffn_conv_w, ffn_conv_b, ffn_down, ln_ffn_g, ln_ffn_b,
           ple_proj, ple_gate_w, ple_gate_b):
    assert pool_w.shape[0] == 1 and ssm_in_proj.shape[0] == 1 and ffn_up.shape[0] == DEPTH
    w = dict(pool_w=pool_w, pool_scale=pool_scale, ssm_in_proj=ssm_in_proj, ssm_conv_w=ssm_conv_w,
             ssm_conv_b=ssm_conv_b, ssm_dt_bias=ssm_dt_bias, ssm_A_log=ssm_A_log, ssm_D=ssm_D,
             ssm_norm_w=ssm_norm_w, ssm_out_proj=ssm_out_proj, ln_mix_g=ln_mix_g, ln_mix_b=ln_mix_b,
             ffn_up=ffn_up, ffn_conv_w=ffn_conv_w, ffn_conv_b=ffn_conv_b, ffn_down=ffn_down,
             ln_ffn_g=ln_ffn_g, ln_ffn_b=ln_ffn_b, ple_proj=ple_proj, ple_gate_w=ple_gate_w,
             ple_gate_b=ple_gate_b)
    bp, seq, d = x_prompt.shape
    bs, dseq, _ = x_sample.shape
    f32 = x_prompt.dtype
    conv_dim = ssm_conv_w.shape[-1]
    z_pool = jnp.zeros((1, bp, POOL_HIST, d), f32)
    z_sconv = jnp.zeros((1, bp, SSM_CONV - 1, conv_dim), f32)
    z_state = jnp.zeros((1, bp, N_SSM_HEADS, HEAD_DIM, D_STATE), f32)
    z_ffn = jnp.zeros((DEPTH, bp, FFN_CONV - 1, ffn_conv_w.shape[-1]), f32)
    tl_p = _block_len(seq, 512)
    tl_s = _block_len(seq, 256)
    y_p, pool_p, sconv_p, state_p, ffn_p = _trunk(
        x_prompt, p_prompt, z_pool, z_sconv, z_state, z_ffn, w, pos0=0, q=min(SSD_BLOCK, seq),
        pool_blk=(1, tl_p), ssd_blk=(1, tl_s), ffn_blk=(1, tl_p, _phases(tl_p)))
    y_s, pool_s, sconv_s, state_s, ffn_s = _trunk(
        x_sample, p_sample, cache_pool, cache_ssm_conv, state_ssm, cache_ffn_conv, w, pos0=PAST_LEN,
        q=dseq, pool_blk=(bs, dseq), ssd_blk=(4 if bs % 4 == 0 else 1, dseq),
        ffn_blk=(bs, dseq, _phases(dseq)))
    return (y_p, y_s, pool_p, pool_s, sconv_p, sconv_s, state_p, state_s, ffn_p, ffn_s)
```
